```python
import math
import jax
import jax.numpy as jnp
from jax import lax
import numpy as np

D_MODEL = 1024
BATCH = 8
SEQ = 8192
DEPTH = 2

N_A_LAYERS = DEPTH // 2
N_B_LAYERS = DEPTH - N_A_LAYERS
N_DENSE_LAYERS = (DEPTH + 1) // 2
N_MOE_LAYERS = DEPTH // 2

HGRN_EXPAND = 128
HGRN_HEADS = D_MODEL // HGRN_EXPAND
HGRN_DK = HGRN_EXPAND
HGRN_DV = D_MODEL // HGRN_HEADS
HGRN_CHUNK = 64

SWA_HEAD_DIM = 64
SWA_Q_HEADS = D_MODEL // SWA_HEAD_DIM
SWA_KV_HEADS = 4
SWA_GROUP = SWA_Q_HEADS // SWA_KV_HEADS
WINDOW = 128

REL_BUCKETS = 32
REL_MAX_DIST = 128

D_FF_DENSE = 2816
N_EXPERTS = 8
TOP_K = 2
D_FF_EXPERT = 3584
MOE_BLOCK = 256

NORM_EPS = 1e-6

kernel_name = 'hybrid_hgrn2_swa_yoco_moe'


def rmsnorm(x, gain):
    xf = x.astype(jnp.float32)
    y = xf * lax.rsqrt(jnp.mean(xf * xf, axis=-1, keepdims=True) + NORM_EPS)
    return (y * gain.astype(jnp.float32)).astype(x.dtype)


def hgrn2_mixer(hn, w_in, lb, g_norm, w_out):
    B, S, _ = hn.shape
    nc = S // HGRN_CHUNK
    proj = hn @ w_in
    q, f_raw, i_in, o_gate = jnp.split(proj, 4, axis=-1)
    lbf = lb.astype(jnp.float32)
    f = lbf + (1.0 - lbf) * jax.nn.sigmoid(f_raw.astype(jnp.float32))
    log_f = jnp.log(f)
    k = 1.0 - f

    def to_chunks(t, d):
        return t.astype(jnp.float32).reshape(B, nc, HGRN_CHUNK, HGRN_HEADS, d).transpose(1, 0, 3, 2, 4)

    qc = to_chunks(q, HGRN_DK)
    kc = to_chunks(k, HGRN_DK)
    vc = to_chunks(i_in, HGRN_DV)
    gc = to_chunks(log_f, HGRN_DK)
    causal = jnp.tril(jnp.ones((HGRN_CHUNK, HGRN_CHUNK), dtype=bool))

    def chunk_step(state, inp):
        qt, kt, vt, gt = inp
        b = jnp.cumsum(gt, axis=2)
        rel = b[:, :, :, None, :] - b[:, :, None, :, :]
        decay = jnp.exp(jnp.where(causal[:, :, None], rel, -jnp.inf))
        scores = jnp.einsum('bhtd,bhsd,bhtsd->bhts', qt, kt, decay)
        o = scores @ vt + jnp.einsum('bhtd,bhde->bhte', qt * jnp.exp(b), state)
        b_last = b[:, :, -1:, :]
        new_state = jnp.exp(b_last[:, :, 0, :])[..., None] * state + jnp.einsum(
            'bhsd,bhse->bhde', kt * jnp.exp(b_last - b), vt)
        return new_state, o

    s0 = jnp.zeros((B, HGRN_HEADS, HGRN_DK, HGRN_DV), jnp.float32)
    _, o = lax.scan(chunk_step, s0, (qc, kc, vc, gc))
    o = o.transpose(1, 0, 3, 2, 4).reshape(B, S, HGRN_HEADS, HGRN_DV)
    o = rmsnorm(o, g_norm).reshape(B, S, HGRN_HEADS * HGRN_DV)
    o = o * jax.nn.silu(o_gate.astype(jnp.float32))
    return o.astype(hn.dtype) @ w_out


def shared_kv(h, kv_norm, kv_w, k_gain):
    B, S, _ = h.shape
    nb = S // WINDOW
    hn = rmsnorm(h, kv_norm)
    kv = hn @ kv_w
    k, v = jnp.split(kv, 2, axis=-1)
    k = rmsnorm(k.reshape(B, S, SWA_KV_HEADS, SWA_HEAD_DIM), k_gain)
    v = v.reshape(B, S, SWA_KV_HEADS, SWA_HEAD_DIM)

    def band(t):
        tp = jnp.pad(t.astype(jnp.float32), ((0, 0), (WINDOW, 0), (0, 0), (0, 0)))
        tb = tp.reshape(B, nb + 1, WINDOW, SWA_KV_HEADS, SWA_HEAD_DIM)
        return jnp.concatenate([tb[:, :-1], tb[:, 1:]], axis=2)

    return band(k), band(v)


def rel_bucket(n):
    max_exact = REL_BUCKETS // 2
    nf = jnp.maximum(n, 1).astype(jnp.float32)
    large = max_exact + (jnp.log(nf / max_exact) / math.log(REL_MAX_DIST / max_exact)
                         * (REL_BUCKETS - max_exact)).astype(jnp.int32)
    large = jnp.minimum(large, REL_BUCKETS - 1)
    return jnp.where(n < max_exact, n, large)


def swa_attention(hn, k_band, v_band, w_q, q_gain, sink, rel_bias, w_o):
    B, S, _ = hn.shape
    nb = S // WINDOW
    q = (hn @ w_q).reshape(B, S, SWA_Q_HEADS, SWA_HEAD_DIM)
    q = rmsnorm(q, q_gain)
    qb = q.astype(jnp.float32).reshape(B, nb, WINDOW, SWA_KV_HEADS, SWA_GROUP, SWA_HEAD_DIM)
    scores = jnp.einsum('bnqkgd,bnskd->bnkgqs', qb, k_band) * (SWA_HEAD_DIM ** -0.5)
    qi = jnp.arange(WINDOW)[:, None]
    kj = jnp.arange(2 * WINDOW)[None, :]
    dist = qi + WINDOW - kj
    in_win = (dist >= 0) & (dist < WINDOW)
    bias = rel_bias.astype(jnp.float32)[rel_bucket(jnp.maximum(dist, 0))]
    bias = bias.transpose(2, 0, 1).reshape(SWA_KV_HEADS, SWA_GROUP, WINDOW, 2 * WINDOW)
    first_ok = (jnp.arange(nb)[:, None] > 0) | (jnp.arange(2 * WINDOW)[None, :] >= WINDOW)
    valid = in_win[None, :, :] & first_ok[:, None, :]
    logits = jnp.where(valid[None, :, None, None], scores + bias, -jnp.inf)
    sink_l = sink.astype(jnp.float32).reshape(SWA_KV_HEADS, SWA_GROUP)[None, None, :, :, None, None]
    m = jnp.maximum(jnp.max(logits, axis=-1, keepdims=True), sink_l)
    p = jnp.exp(logits - m)
    probs = p / (jnp.sum(p, axis=-1, keepdims=True) + jnp.exp(sink_l - m))
    out = jnp.einsum('bnkgqs,bnskd->bnqkgd', probs, v_band).reshape(B, S, SWA_Q_HEADS * SWA_HEAD_DIM)
    return out.astype(hn.dtype) @ w_o


def swiglu(hn, w_gate, w_up, w_down):
    return (jax.nn.silu(hn @ w_gate) * (hn @ w_up)) @ w_down


def moe_swiglu(hn, w_router, w_gate, w_up, w_down):
    B, S, D = hn.shape
    T = B * S
    A = T * TOP_K
    xt = hn.reshape(T, D)
    logits = (xt @ w_router).astype(jnp.float32)
    top_logit, top_idx = lax.top_k(logits, TOP_K)
    gates = jax.nn.softmax(top_logit, axis=-1)
    flat_e = top_idx.reshape(A)
    flat_tok = jnp.repeat(jnp.arange(T, dtype=jnp.int32), TOP_K)
    flat_g = gates.reshape(A)
    order = jnp.argsort(flat_e, stable=True)
    s_e = flat_e[order]
    s_tok = flat_tok[order]
    s_g = flat_g[order]
    counts = jnp.bincount(flat_e, length=N_EXPERTS)
    padded = (counts + MOE_BLOCK - 1) // MOE_BLOCK * MOE_BLOCK
    pad_end = jnp.cumsum(padded)
    pad_start = pad_end - padded
    grp_start = jnp.cumsum(counts) - counts
    dest = pad_start[s_e] + jnp.arange(A) - grp_start[s_e]
    m_pad = A + N_EXPERTS * MOE_BLOCK
    n_blk = m_pad // MOE_BLOCK
    row_tok = jnp.zeros((m_pad,), jnp.int32).at[dest].set(s_tok)
    row_g = jnp.zeros((m_pad,), jnp.float32).at[dest].set(s_g)
    blk_e = jnp.minimum(jnp.searchsorted(pad_end, jnp.arange(n_blk) * MOE_BLOCK, side='right'),
                        N_EXPERTS - 1)
    xs = xt[row_tok].reshape(n_blk, MOE_BLOCK, D)

    def expert_block(args):
        xb, e = args
        hid = jax.nn.silu(xb @ w_gate[e]) * (xb @ w_up[e])
        return hid @ w_down[e]

    yb = lax.map(expert_block, (xs, blk_e)).reshape(m_pad, D)
    y = jnp.zeros((T, D), hn.dtype).at[row_tok].add(
        (yb.astype(jnp.float32) * row_g[:, None]).astype(hn.dtype))
    return y.reshape(B, S, D)


def setup_inputs(seed: int = 0) -> dict:
    key = jax.random.key(seed)
    ks = jax.random.split(key, 24)
    D = D_MODEL
    f32 = jnp.float32

    def w(k, shape, fan_in):
        return jax.random.normal(k, shape, f32) * (fan_in ** -0.5)

    def gain(k, shape):
        return 1.0 + 0.05 * jax.random.normal(k, shape, f32)

    q_width = SWA_Q_HEADS * SWA_HEAD_DIM
    kv_width = 2 * SWA_KV_HEADS * SWA_HEAD_DIM
    return {
        'x': jax.random.normal(ks[0], (BATCH, SEQ, D), f32),
        'hgrn_w_in': w(ks[1], (N_A_LAYERS, D, 4 * D), D),
        'hgrn_lb': 0.1 * jax.random.normal(ks[2], (DEPTH + 1, D), f32),
        'hgrn_gnorm': gain(ks[3], (N_A_LAYERS, HGRN_DV)),
        'hgrn_w_out': w(ks[4], (N_A_LAYERS, D, D), D),
        'swa_w_q': w(ks[5], (N_B_LAYERS, D, q_width), D),
        'swa_q_gain': gain(ks[6], (N_B_LAYERS, SWA_HEAD_DIM)),
        'swa_sink': jax.random.normal(ks[7], (N_B_LAYERS, SWA_Q_HEADS), f32),
        'swa_w_o': w(ks[8], (N_B_LAYERS, q_width, D), q_width),
        'kv_norm': gain(ks[9], (D,)),
        'kv_w': w(ks[10], (D, kv_width), D),
        'k_gain': gain(ks[11], (SWA_HEAD_DIM,)),
        'rel_bias': 0.5 * jax.random.normal(ks[12], (REL_BUCKETS, SWA_Q_HEADS), f32),
        'attn_norm': gain(ks[13], (DEPTH, D)),
        'ffn_norm': gain(ks[14], (DEPTH, D)),
        'ffn_w_gate': w(ks[15], (N_DENSE_LAYERS, D, D_FF_DENSE), D),
        'ffn_w_up': w(ks[16], (N_DENSE_LAYERS, D, D_FF_DENSE), D),
        'ffn_w_down': w(ks[17], (N_DENSE_LAYERS, D_FF_DENSE, D), D_FF_DENSE),
        'moe_router': w(ks[18], (N_MOE_LAYERS, D, N_EXPERTS), D),
        'moe_w_gate': w(ks[19], (N_MOE_LAYERS, N_EXPERTS, D, D_FF_EXPERT), D),
        'moe_w_up': w(ks[20], (N_MOE_LAYERS, N_EXPERTS, D, D_FF_EXPERT), D),
        'moe_w_down': w(ks[21], (N_MOE_LAYERS, N_EXPERTS, D_FF_EXPERT, D), D_FF_EXPERT),
    }


def reference(x, hgrn_w_in, hgrn_lb, hgrn_gnorm, hgrn_w_out, swa_w_q, swa_q_gain, swa_sink,
              swa_w_o, kv_norm, kv_w, k_gain, rel_bias, attn_norm, ffn_norm, ffn_w_gate,
              ffn_w_up, ffn_w_down, moe_router, moe_w_gate, moe_w_up, moe_w_down):
    h = x
    lb_all = jnp.cumsum(jax.nn.softmax(hgrn_lb.astype(jnp.float32), axis=0), axis=0)
    k_band = None
    v_band = None
    for l in range(DEPTH):
        hn = rmsnorm(h, attn_norm[l])
        if l < N_A_LAYERS:
            a = l
            h = h + hgrn2_mixer(hn, hgrn_w_in[a], lb_all[l], hgrn_gnorm[a], hgrn_w_out[a])
        else:
            bi = l - N_A_LAYERS
            h = h + swa_attention(hn, k_band, v_band, swa_w_q[bi], swa_q_gain[bi], swa_sink[bi],
                                  rel_bias, swa_w_o[bi])
        hn = rmsnorm(h, ffn_norm[l])
        if l % 2 == 0:
            h = h + swiglu(hn, ffn_w_gate[l // 2], ffn_w_up[l // 2], ffn_w_down[l // 2])
        else:
            h = h + moe_swiglu(hn, moe_router[l // 2], moe_w_gate[l // 2], moe_w_up[l // 2],
                               moe_w_down[l // 2])
        if l == N_A_LAYERS - 1:
            k_band, v_band = shared_kv(h, kv_norm, kv_w, k_gain)
    return h
```

```python
import functools

import numpy as np
import jax
import jax.numpy as jnp
from jax import lax
from jax.experimental import pallas as pl
from jax.experimental.pallas import tpu as pltpu

F32 = jnp.float32
BF16 = jnp.bfloat16

D_MODEL = 1024
NORM_EPS = 1e-6

HG_HEADS = 8
HG_DK = 128
HG_DV = 128
HG_CHUNK = 128
HG_LEVELS = 7

VMEM_LIMIT_BYTES = 56 * 1024 * 1024


def _dot(a, b):
    return jnp.dot(a, b, preferred_element_type=F32)


def _dot_nt(a, b):
    return lax.dot_general(a, b, (((1,), (1,)), ((), ())), preferred_element_type=F32)


def _dot_tn(a, b):
    return lax.dot_general(a, b, (((0,), (0,)), ((), ())), preferred_element_type=F32)


def _rms_scale(x):
    return lax.rsqrt(jnp.mean(x * x, axis=-1, keepdims=True) + NORM_EPS)


def _sigmoid(x):
    return 1.0 / (1.0 + jnp.exp(-x))


def _const_spec(shape):
    nd = len(shape)
    return pl.BlockSpec(shape, lambda *_: (0,) * nd, pipeline_mode=pl.Buffered(1))


def _hgrn_decay_sums():
    c = HG_CHUNK
    t = np.arange(c)[:, None]
    u = np.arange(c)[None, :]
    blocks = [(u <= t), (u > t)]
    for l in range(HG_LEVELS):
        m = 1 << l
        mid = (t // (2 * m)) * (2 * m) + m - 1
        upper = (t > mid) & (u > mid) & (u <= t)
        lower = (t <= mid) & (u > t) & (u <= mid)
        blocks.append(upper | lower)
    return np.concatenate(blocks, axis=0).astype(np.float32)


def _hgrn_kernel(x_ref, gain_ref, win_ref, lbp_ref, tsum_ref, gnorm_ref, wout_ref,
                 out_ref, st_ref, o_ref):
    c = HG_CHUNK

    @pl.when(pl.program_id(1) == 0)
    def _():
        st_ref[...] = jnp.zeros_like(st_ref)

    x = x_ref[0]
    hn = (x * _rms_scale(x) * gain_ref[...]).astype(BF16)
    proj = _dot(hn, win_ref[...])
    q = proj[:, 0 * D_MODEL:1 * D_MODEL]
    f_raw = proj[:, 1 * D_MODEL:2 * D_MODEL]
    v = proj[:, 2 * D_MODEL:3 * D_MODEL].astype(BF16)
    o_gate = proj[:, 3 * D_MODEL:4 * D_MODEL]

    lbp = lbp_ref[...]
    lbe = jnp.exp(lbp - jnp.max(lbp, axis=0, keepdims=True))
    lb = lbe[0:1] / jnp.sum(lbe, axis=0, keepdims=True)

    f = lb + (1.0 - lb) * _sigmoid(f_raw)
    g = jnp.log(f)
    k = 1.0 - f

    g_hi = g.astype(BF16)
    g_lo = (g - g_hi.astype(F32)).astype(BF16)
    sums = _dot(tsum_ref[...], jnp.concatenate([g_hi, g_lo], axis=0))
    b = sums[0:c]
    b_last = b[c - 1:c]
    q_bf = q.astype(BF16)
    k_bf = k.astype(BF16)
    qb = (q * jnp.exp(b)).astype(BF16)
    khat = (k * jnp.exp(sums[c:2 * c])).astype(BF16)
    decay = jnp.exp(b_last)
    q_lv, k_lv = [], []
    for l in range(HG_LEVELS):
        w = jnp.exp(sums[(2 + l) * c:(3 + l) * c])
        q_lv.append((q * w).astype(BF16))
        k_lv.append((k * w).astype(BF16))

    ti = lax.broadcasted_iota(jnp.int32, (c, c), 0)
    si = lax.broadcasted_iota(jnp.int32, (c, c), 1)
    diag = ti == si
    xor = ti ^ si
    lv_mask = [((xor >> l) == 1) & (((ti >> l) & 1) == 1) for l in range(HG_LEVELS)]

    for h in range(HG_HEADS):
        hs = slice(h * HG_DK, (h + 1) * HG_DK)
        a = jnp.where(diag, _dot_nt(q_bf[:, hs], k_bf[:, hs]), 0.0)
        for l in range(HG_LEVELS):
            a = jnp.where(lv_mask[l], _dot_nt(q_lv[l][:, hs], k_lv[l][:, hs]), a)
        st = st_ref[h]
        o_h = _dot(a.astype(BF16), v[:, hs]) + _dot_nt(qb[:, hs], st.astype(BF16))
        st_ref[h] = st * decay[:, hs] + _dot_tn(v[:, hs], khat[:, hs])
        o_h = o_h * _rms_scale(o_h) * gnorm_ref[...]
        og = o_gate[:, hs]
        o_ref[:, hs] = (o_h * (og * _sigmoid(og))).astype(BF16)

    out_ref[0] = x + _dot(o_ref[...], wout_ref[...])


def _hgrn_layer(x, gain, w_in, lb_param, gnorm, w_out):
    bsz, seq, d = x.shape
    c = HG_CHUNK
    tsum = _hgrn_decay_sums()
    tsum2 = jnp.asarray(np.concatenate([tsum, tsum], axis=1), dtype=BF16)
    return pl.pallas_call(
        _hgrn_kernel,
        out_shape=jax.ShapeDtypeStruct((bsz, seq, d), F32),
        grid=(bsz, seq // c),
        in_specs=[
            pl.BlockSpec((1, c, d), lambda b, j: (b, j, 0)),
            _const_spec((1, d)),
            _const_spec((d, 4 * d)),
            _const_spec(lb_param.shape),
            _const_spec(tsum2.shape),
            _const_spec((1, HG_DV)),
            _const_spec((d, d)),
        ],
        out_specs=pl.BlockSpec((1, c, d), lambda b, j: (b, j, 0)),
        scratch_shapes=[
            pltpu.VMEM((HG_HEADS, HG_DV, HG_DK), F32),
            pltpu.VMEM((c, d), BF16),
        ],
        compiler_params=pltpu.CompilerParams(
            dimension_semantics=("arbitrary", "arbitrary"),
            vmem_limit_bytes=VMEM_LIMIT_BYTES),
        name="hgrn_layer",
    )(x, gain.reshape(1, d), w_in, lb_param, tsum2, gnorm.reshape(1, HG_DV), w_out)


D_FF_DENSE = 2816
FFN_FF_CHUNKS = 2
FFN_ROWS = 512

SWA_HEAD_DIM = 64
SWA_Q_HEADS = 16
SWA_KV_HEADS = 4
SWA_WINDOW = 128
SWA_KV_LANES = 2 * SWA_HEAD_DIM


def _ffn_kvq_kernel(h_ref, fgain_ref, wg_ref, wu_ref, wd_ref, kvgain_ref, kvw_ref, kgain_ref,
                    qgain_ref, wq_ref, h2_ref, q_ref, k_ref, v_ref):
    h = h_ref[...]
    hn = (h * _rms_scale(h) * fgain_ref[...]).astype(BF16)
    acc = h
    fc = D_FF_DENSE // FFN_FF_CHUNKS
    for c in range(FFN_FF_CHUNKS):
        sl = slice(c * fc, (c + 1) * fc)
        gate = _dot(hn, wg_ref[:, sl])
        up = _dot(hn, wu_ref[:, sl])
        hid = (gate * _sigmoid(gate) * up).astype(BF16)
        acc = acc + _dot(hid, wd_ref[sl, :])
    h2_ref[...] = acc
    normed = acc * _rms_scale(acc)
    kv = _dot((normed * kvgain_ref[...]).astype(BF16), kvw_ref[...])
    nk = SWA_KV_HEADS * SWA_KV_LANES
    for g in range(SWA_KV_HEADS):
        sl = slice(g * SWA_KV_LANES, (g + 1) * SWA_KV_LANES)
        kg = kv[:, sl]
        k_ref[:, sl] = (kg * _rms_scale(kg) * kgain_ref[...]).astype(BF16)
    v_ref[...] = kv[:, nk:].astype(BF16)
    q_ref[...] = _dot((normed * qgain_ref[...]).astype(BF16), wq_ref[...]).astype(BF16)


def _dup_kv_columns(w):
    d = w.shape[0]
    w = w.reshape(d, SWA_KV_HEADS, 1, SWA_HEAD_DIM)
    return jnp.broadcast_to(w, (d, SWA_KV_HEADS, 2, SWA_HEAD_DIM)).reshape(d, SWA_KV_HEADS * SWA_KV_LANES)


def _ffn_kvq(h1, fgain, wg, wu, wd, kvgain, kv_w, k_gain, qgain, wq):
    t, d = h1.shape
    rows = min(FFN_ROWS, t)
    nkv = SWA_KV_HEADS * SWA_HEAD_DIM
    kvw2 = jnp.concatenate([_dup_kv_columns(kv_w[:, :nkv]), _dup_kv_columns(kv_w[:, nkv:])],
                           axis=1).astype(BF16)
    kgain2 = jnp.concatenate([k_gain, k_gain]).reshape(1, SWA_KV_LANES)
    nk = SWA_KV_HEADS * SWA_KV_LANES
    row_spec = lambda w: pl.BlockSpec((rows, w), lambda i: (i, 0))
    return pl.pallas_call(
        _ffn_kvq_kernel,
        out_shape=(jax.ShapeDtypeStruct((t, d), F32), jax.ShapeDtypeStruct((t, d), BF16),
                   jax.ShapeDtypeStruct((t, nk), BF16), jax.ShapeDtypeStruct((t, nk), BF16)),
        grid=(t // rows,),
        in_specs=[
            row_spec(d),
            _const_spec((1, d)),
            _const_spec((d, D_FF_DENSE)), _const_spec((d, D_FF_DENSE)), _const_spec((D_FF_DENSE, d)),
            _const_spec((1, d)), _const_spec((d, 2 * nk)), _const_spec((1, SWA_KV_LANES)),
            _const_spec((1, d)), _const_spec((d, d)),
        ],
        out_specs=(row_spec(d), row_spec(d), row_spec(nk), row_spec(nk)),
        compiler_params=pltpu.CompilerParams(
            dimension_semantics=("arbitrary",), vmem_limit_bytes=VMEM_LIMIT_BYTES),
        name="ffn_kvq",
    )(h1, fgain.reshape(1, d), wg, wu, wd, kvgain.reshape(1, d), kvw2, kgain2,
      qgain.reshape(1, d), wq)


ATTN_ROWS = 256
REL_BUCKETS = 32
REL_MAX_DIST = 128
N_EXPERTS = 8
ROUTE_LANES = 128
ROUTE_COLS = 8
ROW_CHUNKS = D_MODEL // 128


def _rel_bucket_table():
    w = SWA_WINDOW
    qi = np.arange(w)[:, None]
    kj = np.arange(2 * w)[None, :]
    dist = qi + w - kj
    in_win = (dist >= 0) & (dist < w)
    n = np.maximum(dist, 0)
    max_exact = REL_BUCKETS // 2
    nf = np.maximum(n, 1).astype(np.float64)
    large = max_exact + (np.log(nf / max_exact) / np.log(REL_MAX_DIST / max_exact)
                         * (REL_BUCKETS - max_exact)).astype(np.int64)
    large = np.minimum(large, REL_BUCKETS - 1)
    bucket = np.where(n < max_exact, n, large)
    return np.where(in_win, bucket, -1).astype(np.int32)


def _attn_router_kernel(relb_ref, sink_ref, q_ref, kc_ref, kp_ref, vc_ref, vp_ref, h2_ref,
                        bucket_ref, qgain_ref, wo_ref, fgain_ref, wr_ref,
                        h3_ref, hn2_ref, route_ref, bias_ref, attn_ref):
    w = SWA_WINDOW
    hd = SWA_HEAD_DIM
    rows = q_ref.shape[1]
    first_tile = pl.program_id(1) == 0

    @pl.when(jnp.logical_and(pl.program_id(0) == 0, first_tile))
    def _():
        bucket = bucket_ref[...]
        for h in range(SWA_Q_HEADS):
            plane = jnp.full((w, 2 * w), -jnp.inf, F32)
            for bkt in range(REL_BUCKETS):
                plane = jnp.where(bucket == bkt, relb_ref[bkt * SWA_Q_HEADS + h], plane)
            bias_ref[h] = plane

    lane = lax.broadcasted_iota(jnp.int32, (2 * w, SWA_KV_LANES), 1)
    left = lane < hd
    lane_q = lax.broadcasted_iota(jnp.int32, (w, SWA_KV_LANES), 1)
    left_q = lane_q < hd
    key_idx = lax.broadcasted_iota(jnp.int32, (1, 2 * w), 1)
    qgain = qgain_ref[...]

    k_all = jnp.concatenate([kp_ref[0], kc_ref[0]], axis=0)
    v_all = jnp.concatenate([vp_ref[0], vc_ref[0]], axis=0)
    zero = jnp.zeros((), BF16)
    for blk in range(rows // w):
        r0 = blk * w
        if blk == 0:
            no_prev = jnp.where(jnp.logical_and(first_tile, key_idx < w), -jnp.inf, 0.0)
        else:
            no_prev = None
        for g in range(SWA_KV_HEADS):
            ks = slice(g * SWA_KV_LANES, (g + 1) * SWA_KV_LANES)
            k2 = k_all[r0:r0 + 2 * w, ks]
            v2 = v_all[r0:r0 + 2 * w, ks]
            k_bd = jnp.concatenate([jnp.where(left, k2, zero), jnp.where(left, zero, k2)], axis=0)
            v_bd = jnp.concatenate([jnp.where(left, v2, zero), jnp.where(left, zero, v2)], axis=0)
            pairs = []
            for p in range(2):
                c0 = (4 * g + 2 * p) * hd
                qp = q_ref[0, r0:r0 + w, c0:c0 + 2 * hd].astype(F32)
                sq = qp * qp
                s_l = jnp.sum(jnp.where(left_q, sq, 0.0), axis=-1, keepdims=True)
                s_r = jnp.sum(jnp.where(left_q, 0.0, sq), axis=-1, keepdims=True)
                ms = jnp.where(left_q, s_l, s_r) * (1.0 / hd)
                pairs.append((qp * lax.rsqrt(ms + NORM_EPS) * qgain * (hd ** -0.5)).astype(BF16))
            qs = jnp.concatenate(pairs, axis=0)
            s = _dot_nt(qs, k_bd)
            outs = []
            for p in range(2):
                halves = []
                for side in range(2):
                    hq = 4 * g + 2 * p + side
                    logit = s[p * w:(p + 1) * w, side * 2 * w:(side + 1) * 2 * w] + bias_ref[hq]
                    if no_prev is not None:
                        logit = logit + no_prev
                    sink = sink_ref[hq]
                    m = jnp.maximum(jnp.max(logit, axis=-1, keepdims=True), sink)
                    e = jnp.exp(logit - m)
                    den = jnp.sum(e, axis=-1, keepdims=True) + jnp.exp(sink - m)
                    halves.append((e, den))
                e2 = jnp.concatenate([halves[0][0], halves[1][0]], axis=1).astype(BF16)
                pv = _dot(e2, v_bd)
                pv = pv * jnp.where(left_q, 1.0 / halves[0][1], 1.0 / halves[1][1])
                c0 = (4 * g + 2 * p) * hd
                attn_ref[r0:r0 + w, c0:c0 + 2 * hd] = pv.astype(BF16)

    h3 = h2_ref[0] + _dot(attn_ref[...], wo_ref[...])
    h3_ref[0] = h3
    hn2 = h3 * _rms_scale(h3) * fgain_ref[...]
    for c in range(ROW_CHUNKS):
        hn2_ref[0, :, c, :] = hn2[:, c * 128:(c + 1) * 128]

    hi = hn2.astype(BF16)
    lo = (hn2 - hi.astype(F32)).astype(BF16)
    logits = _dot(hi, wr_ref[0]) + _dot(lo, wr_ref[0]) + _dot(hi, wr_ref[1])
    col = lax.broadcasted_iota(jnp.int32, logits.shape, 1)
    logits = jnp.where(col < N_EXPERTS, logits, -jnp.inf)
    m0 = jnp.max(logits, axis=-1, keepdims=True)
    i0 = jnp.min(jnp.where(logits == m0, col, ROUTE_LANES), axis=-1, keepdims=True)
    rest = jnp.where(col == i0, -jnp.inf, logits)
    m1 = jnp.max(rest, axis=-1, keepdims=True)
    i1 = jnp.min(jnp.where(rest == m1, col, ROUTE_LANES), axis=-1, keepdims=True)
    e1 = jnp.exp(m1 - m0)
    g0 = 1.0 / (1.0 + e1)
    g1 = e1 / (1.0 + e1)
    rc = lax.broadcasted_iota(jnp.int32, (rows, ROUTE_COLS), 1)
    rec = jnp.where(rc == 0, i0.astype(F32),
                    jnp.where(rc == 1, i1.astype(F32),
                              jnp.where(rc == 2, g0, jnp.where(rc == 3, g1, 0.0))))
    route_ref[0] = rec


def _attn_router(q, k2, v2, h2, rel_bias, sink, q_gain, w_o, fgain, w_router):
    bsz, seq, d = h2.shape
    rows = min(ATTN_ROWS, seq)
    w = SWA_WINDOW
    nk = SWA_KV_HEADS * SWA_KV_LANES
    per = rows // w
    bucket = jnp.asarray(_rel_bucket_table())
    qgain2 = jnp.concatenate([q_gain, q_gain]).reshape(1, SWA_KV_LANES)
    wr = jnp.zeros((d, ROUTE_LANES), F32).at[:, :N_EXPERTS].set(w_router)
    wr_hi = wr.astype(BF16)
    wr_lo = (wr - wr_hi.astype(F32)).astype(BF16)
    wr2 = jnp.stack([wr_hi, wr_lo])
    tile = lambda width: pl.BlockSpec((1, rows, width), lambda b, j, *_: (b, j, 0))
    prev = pl.BlockSpec((1, w, nk), lambda b, j, *_: (b, jnp.maximum(j * per - 1, 0), 0))
    const = lambda shape: pl.BlockSpec(shape, lambda b, j, *_: (0,) * len(shape),
                                       pipeline_mode=pl.Buffered(1))
    return pl.pallas_call(
        _attn_router_kernel,
        out_shape=(jax.ShapeDtypeStruct((bsz, seq, d), F32),
                   jax.ShapeDtypeStruct((bsz, seq, ROW_CHUNKS, 128), F32),
                   jax.ShapeDtypeStruct((bsz, seq, ROUTE_COLS), F32)),
        grid_spec=pltpu.PrefetchScalarGridSpec(
            num_scalar_prefetch=2,
            grid=(bsz, seq // rows),
            in_specs=[tile(d), tile(nk), prev, tile(nk), prev, tile(d),
                      const((w, 2 * w)), const((1, SWA_KV_LANES)), const((d, d)), const((1, d)),
                      const((2, d, ROUTE_LANES))],
            out_specs=(tile(d), pl.BlockSpec((1, rows, ROW_CHUNKS, 128), lambda b, j, *_: (b, j, 0, 0)),
                       tile(ROUTE_COLS)),
            scratch_shapes=[pltpu.VMEM((SWA_Q_HEADS, w, 2 * w), F32), pltpu.VMEM((rows, d), BF16)],
        ),
        compiler_params=pltpu.CompilerParams(
            dimension_semantics=("arbitrary", "arbitrary"), vmem_limit_bytes=VMEM_LIMIT_BYTES),
        name="attn_router",
    )(rel_bias.reshape(-1), sink, q, k2, k2, v2, v2, h2, bucket, qgain2, w_o, fgain.reshape(1, d), wr2)


TOP_K = 2
D_FF_EXPERT = 3584
MOE_TILE = 1024
MOE_FF_CHUNK = 896
RANK_BLOCK = 1024
DISPATCH_BLOCK = 1024
COMBINE_BLOCK = 512
ZERO_ROWS = 256
RANK_SUBLANES = 16


def _rank_kernel(e_ref, tri_ref, rank_ref, cnt_ref, carry_ref):
    bt = e_ref.shape[1]

    @pl.when(pl.program_id(0) == 0)
    def _():
        carry_ref[...] = jnp.zeros_like(carry_ref)

    ex = lax.broadcasted_iota(jnp.int32, (RANK_SUBLANES, bt), 0)
    oh0 = ex == e_ref[0:1, :]
    oh1 = ex == e_ref[1:2, :]
    both = oh0.astype(F32) + oh1.astype(F32)
    pos = _dot(both.astype(BF16), tri_ref[...]) + carry_ref[:, 0:1]
    rank_ref[0:1, :] = jnp.sum(jnp.where(oh0, pos, 0.0), axis=0, keepdims=True).astype(jnp.int32)
    rank_ref[1:2, :] = jnp.sum(jnp.where(oh1, pos, 0.0), axis=0, keepdims=True).astype(jnp.int32)
    carry_ref[...] = carry_ref[...] + jnp.sum(both, axis=1, keepdims=True)
    cnt_ref[...] = carry_ref[...]


def _route_ranks(experts):
    _, t = experts.shape
    bt = min(RANK_BLOCK, t)
    tri = jnp.asarray(np.triu(np.ones((bt, bt), np.float32), k=1), dtype=BF16)
    rank, cnt = pl.pallas_call(
        _rank_kernel,
        out_shape=(jax.ShapeDtypeStruct((TOP_K, t), jnp.int32),
                   jax.ShapeDtypeStruct((RANK_SUBLANES, 128), F32)),
        grid=(t // bt,),
        in_specs=[pl.BlockSpec((TOP_K, bt), lambda i: (0, i)), _const_spec((bt, bt))],
        out_specs=(pl.BlockSpec((TOP_K, bt), lambda i: (0, i)),
                   pl.BlockSpec((RANK_SUBLANES, 128), lambda i: (0, 0))),
        scratch_shapes=[pltpu.VMEM((RANK_SUBLANES, 128), F32)],
        compiler_params=pltpu.CompilerParams(dimension_semantics=("arbitrary",)),
        name="route_rank",
    )(experts, tri)
    return rank, cnt[:N_EXPERTS, 0].astype(jnp.int32)


def _row_copy(src, src_row, dst, dst_row, sem):
    return pltpu.make_async_copy(src.at[src_row], dst.at[dst_row], sem)


def _dispatch_kernel(start_ref, zrow_ref, nz_ref, hn_ref, e_ref, rank_ref, xs_ref, dest_ref,
                     dvm_ref, dsm_ref, zero_ref, sem_idx, sem_rows, sem_zero):
    bt = hn_ref.shape[0]
    zero_copies = MOE_TILE // ZERO_ROWS

    def zero_copy(z, c):
        row = zrow_ref[z] + c * ZERO_ROWS
        return pltpu.make_async_copy(zero_ref, xs_ref.at[pl.ds(row, ZERO_ROWS)], sem_zero)

    @pl.when(pl.program_id(0) == 0)
    def _():
        zero_ref[...] = jnp.zeros_like(zero_ref)

        def start(z, carry):
            for c in range(zero_copies):
                zero_copy(z, c).start()
            return carry

        def wait(z, carry):
            for c in range(zero_copies):
                zero_copy(z, c).wait()
            return carry

        lax.fori_loop(0, nz_ref[0], start, 0)
        lax.fori_loop(0, nz_ref[0], wait, 0)

    e = e_ref[...]
    base = jnp.zeros_like(e)
    for x in range(N_EXPERTS):
        base = jnp.where(e == x, start_ref[x], base)
    dest = base + rank_ref[...]
    dest_ref[...] = dest
    dvm_ref[...] = dest
    idx_copy = pltpu.make_async_copy(dvm_ref, dsm_ref, sem_idx)
    idx_copy.start()
    idx_copy.wait()

    def issue(t, carry):
        for c in range(TOP_K):
            _row_copy(hn_ref, t, xs_ref, dsm_ref[c, t], sem_rows).start()
        return carry

    lax.fori_loop(0, bt, issue, 0, unroll=8)
    for c in range(TOP_K):
        pltpu.make_async_copy(hn_ref, xs_ref.at[pl.ds(0, bt)], sem_rows).wait()


def _dispatch(hn, experts, rank, start, zrows, nz, m_pad):
    t = hn.shape[0]
    row = hn.shape[1:]
    bt = min(DISPATCH_BLOCK, t)
    blk = lambda i, *_: (0, i)
    return pl.pallas_call(
        _dispatch_kernel,
        out_shape=(jax.ShapeDtypeStruct((m_pad,) + row, F32), jax.ShapeDtypeStruct((TOP_K, t), jnp.int32)),
        grid_spec=pltpu.PrefetchScalarGridSpec(
            num_scalar_prefetch=3,
            grid=(t // bt,),
            in_specs=[pl.BlockSpec((bt,) + row, lambda i, *_: (i, 0, 0)),
                      pl.BlockSpec((TOP_K, bt), blk), pl.BlockSpec((TOP_K, bt), blk)],
            out_specs=(pl.BlockSpec(memory_space=pl.ANY), pl.BlockSpec((TOP_K, bt), blk)),
            scratch_shapes=[pltpu.VMEM((TOP_K, bt), jnp.int32), pltpu.SMEM((TOP_K, bt), jnp.int32),
                            pltpu.VMEM((ZERO_ROWS,) + row, F32),
                            pltpu.SemaphoreType.DMA, pltpu.SemaphoreType.DMA, pltpu.SemaphoreType.DMA],
        ),
        compiler_params=pltpu.CompilerParams(dimension_semantics=("arbitrary",)),
        name="moe_dispatch",
    )(start, zrows, nz, hn, experts, rank)


def _expert_kernel(te_ref, tv_ref, x_ref, wg_ref, wu_ref, wd_ref, out_ref, xb_ref, acc_ref):
    i = pl.program_id(0)
    f = pl.program_id(1)
    valid = tv_ref[i] == 1
    last = f == pl.num_programs(1) - 1

    @pl.when(jnp.logical_and(valid, f == 0))
    def _():
        for c in range(ROW_CHUNKS):
            xb_ref[:, c * 128:(c + 1) * 128] = x_ref[:, c, :].astype(BF16)
        acc_ref[...] = jnp.zeros_like(acc_ref)

    @pl.when(valid)
    def _():
        xb = xb_ref[...]
        gate = _dot(xb, wg_ref[0])
        up = _dot(xb, wu_ref[0])
        hid = (gate * _sigmoid(gate) * up).astype(BF16)
        acc_ref[...] += _dot(hid, wd_ref[0])

    @pl.when(jnp.logical_and(valid, last))
    def _():
        for c in range(ROW_CHUNKS):
            out_ref[:, c, :] = acc_ref[:, c * 128:(c + 1) * 128]

    @pl.when(jnp.logical_and(jnp.logical_not(valid), last))
    def _():
        out_ref[...] = jnp.zeros_like(out_ref)


def _experts(xs, tile_expert, tile_valid, wg, wu, wd):
    m_pad = xs.shape[0]
    d = D_MODEL
    row_block = pl.BlockSpec((MOE_TILE, ROW_CHUNKS, 128), lambda i, f, te, tv: (i, 0, 0))
    nf = D_FF_EXPERT // MOE_FF_CHUNK
    fidx = lambda i, f, te, tv: jnp.where(tv[i] == 1, f, nf - 1)
    return pl.pallas_call(
        _expert_kernel,
        out_shape=jax.ShapeDtypeStruct(xs.shape, F32),
        grid_spec=pltpu.PrefetchScalarGridSpec(
            num_scalar_prefetch=2,
            grid=(m_pad // MOE_TILE, nf),
            in_specs=[
                row_block,
                pl.BlockSpec((1, d, MOE_FF_CHUNK), lambda i, f, te, tv: (te[i], 0, fidx(i, f, te, tv))),
                pl.BlockSpec((1, d, MOE_FF_CHUNK), lambda i, f, te, tv: (te[i], 0, fidx(i, f, te, tv))),
                pl.BlockSpec((1, MOE_FF_CHUNK, d), lambda i, f, te, tv: (te[i], fidx(i, f, te, tv), 0)),
            ],
            out_specs=row_block,
            scratch_shapes=[pltpu.VMEM((MOE_TILE, d), BF16), pltpu.VMEM((MOE_TILE, d), F32)],
        ),
        compiler_params=pltpu.CompilerParams(
            dimension_semantics=("arbitrary", "arbitrary"), vmem_limit_bytes=VMEM_LIMIT_BYTES),
        name="moe_experts",
    )(tile_expert, tile_valid, xs, wg, wu, wd)


def _combine_kernel(h_ref, route_ref, dest_ref, yb_ref, out_ref, dsm_ref, ybuf_ref, sem_idx, sem_rows):
    bt = h_ref.shape[0]
    idx_copy = pltpu.make_async_copy(dest_ref, dsm_ref, sem_idx)
    idx_copy.start()
    idx_copy.wait()

    def issue(t, carry):
        for c in range(TOP_K):
            _row_copy(yb_ref, dsm_ref[c, t], ybuf_ref.at[c], t, sem_rows).start()
        return carry

    lax.fori_loop(0, bt, issue, 0, unroll=8)
    for c in range(TOP_K):
        pltpu.make_async_copy(yb_ref.at[pl.ds(0, bt)], ybuf_ref.at[c], sem_rows).wait()
    route = route_ref[...]
    g0 = route[:, 2:3]
    g1 = route[:, 3:4]
    for c in range(ROW_CHUNKS):
        sl = slice(c * 128, (c + 1) * 128)
        out_ref[:, sl] = h_ref[:, sl] + g0 * ybuf_ref[0, :, c, :] + g1 * ybuf_ref[1, :, c, :]


def _combine(h3, route, dest, yb):
    t, d = h3.shape
    bt = min(COMBINE_BLOCK, t)
    return pl.pallas_call(
        _combine_kernel,
        out_shape=jax.ShapeDtypeStruct((t, d), F32),
        grid=(t // bt,),
        in_specs=[pl.BlockSpec((bt, d), lambda i: (i, 0)),
                  pl.BlockSpec((bt, ROUTE_COLS), lambda i: (i, 0)),
                  pl.BlockSpec((TOP_K, bt), lambda i: (0, i)),
                  pl.BlockSpec(memory_space=pl.ANY)],
        out_specs=pl.BlockSpec((bt, d), lambda i: (i, 0)),
        scratch_shapes=[pltpu.SMEM((TOP_K, bt), jnp.int32),
                        pltpu.VMEM((TOP_K, bt, ROW_CHUNKS, 128), F32),
                        pltpu.SemaphoreType.DMA, pltpu.SemaphoreType.DMA],
        compiler_params=pltpu.CompilerParams(
            dimension_semantics=("arbitrary",), vmem_limit_bytes=VMEM_LIMIT_BYTES),
        name="moe_combine",
    )(h3, route, dest, yb)


def _moe_layer(h3, hn2, route, wg, wu, wd):
    t, d = h3.shape
    n_tiles = (t * TOP_K) // MOE_TILE + N_EXPERTS
    m_pad = n_tiles * MOE_TILE
    experts = route[:, :TOP_K].astype(jnp.int32).T
    rank, counts = _route_ranks(experts)
    tiles_per = (counts + MOE_TILE - 1) // MOE_TILE
    tile_end = jnp.cumsum(tiles_per)
    start = ((tile_end - tiles_per) * MOE_TILE).astype(jnp.int32)
    tile_ids = jnp.arange(n_tiles, dtype=jnp.int32)
    tile_valid = (tile_ids < tile_end[-1]).astype(jnp.int32)
    last_valid = jnp.maximum(tile_end[-1] - 1, 0)
    tile_expert = jnp.minimum(
        jnp.searchsorted(tile_end, jnp.minimum(tile_ids, last_valid), side="right"),
        N_EXPERTS - 1).astype(jnp.int32)
    has_pad = (counts % MOE_TILE) != 0
    pad_tile = jnp.where(has_pad, tile_end - 1, n_tiles)
    zmask = jnp.zeros((n_tiles + 1,), jnp.int32).at[pad_tile].set(1)[:n_tiles]
    zmask = jnp.maximum(zmask, 1 - tile_valid)
    nz = jnp.sum(zmask).astype(jnp.int32).reshape(1)
    zorder = jnp.argsort(1 - zmask, stable=True).astype(jnp.int32)
    zrows = (zorder[:2 * N_EXPERTS] * MOE_TILE).astype(jnp.int32)
    xs, dest = _dispatch(hn2, experts, rank, start, zrows, nz, m_pad)
    yb = _experts(xs, tile_expert, tile_valid, wg, wu, wd)
    return _combine(h3, route, dest, yb)


def _stages(x, hgrn_w_in, hgrn_lb, hgrn_gnorm, hgrn_w_out, swa_w_q, swa_q_gain, swa_sink, swa_w_o, kv_norm, kv_w, k_gain, rel_bias, attn_norm, ffn_norm, ffn_w_gate, ffn_w_up, ffn_w_down, moe_router, moe_w_gate, moe_w_up, moe_w_down):
    bsz, seq, d = x.shape
    t = bsz * seq
    bf = lambda w: w.astype(BF16)
    h1 = _hgrn_layer(x, attn_norm[0], bf(hgrn_w_in[0]), hgrn_lb, hgrn_gnorm[0], bf(hgrn_w_out[0]))
    h2, q, k2, v2 = _ffn_kvq(h1.reshape(t, d), ffn_norm[0], bf(ffn_w_gate[0]), bf(ffn_w_up[0]),
                             bf(ffn_w_down[0]), kv_norm, kv_w, k_gain, attn_norm[1], bf(swa_w_q[0]))
    nk = SWA_KV_HEADS * SWA_KV_LANES
    h3, hn2, route = _attn_router(q.reshape(bsz, seq, d), k2.reshape(bsz, seq, nk),
                                  v2.reshape(bsz, seq, nk), h2.reshape(bsz, seq, d), rel_bias,
                                  swa_sink[0], swa_q_gain[0], bf(swa_w_o[0]), ffn_norm[1],
                                  moe_router[0])
    h4 = _moe_layer(h3.reshape(t, d), hn2.reshape(t, ROW_CHUNKS, 128), route.reshape(t, ROUTE_COLS),
                    bf(moe_w_gate[0]), bf(moe_w_up[0]), bf(moe_w_down[0]))
    return {"h1": h1, "h2": h2.reshape(bsz, seq, d), "h3": h3, "h4": h4.reshape(bsz, seq, d)}


def kernel(x, hgrn_w_in, hgrn_lb, hgrn_gnorm, hgrn_w_out, swa_w_q, swa_q_gain, swa_sink, swa_w_o, kv_norm, kv_w, k_gain, rel_bias, attn_norm, ffn_norm, ffn_w_gate, ffn_w_up, ffn_w_down, moe_router, moe_w_gate, moe_w_up, moe_w_down):
    return _stages(x, hgrn_w_in, hgrn_lb, hgrn_gnorm, hgrn_w_out, swa_w_q, swa_q_gain, swa_sink, swa_w_o, kv_norm, kv_w, k_gain, rel_bias, attn_norm, ffn_norm, ffn_w_gate, ffn_w_up, ffn_w_down, moe_router, moe_w_gate, moe_w_up, moe_w_down)["h4"]
```

```python
import functools

import numpy as np
import jax
import jax.numpy as jnp
from jax import lax
from jax.experimental import pallas as pl
from jax.experimental.pallas import tpu as pltpu

F32 = jnp.float32
BF16 = jnp.bfloat16

D_MODEL = 1024
NORM_EPS = 1e-6

HG_HEADS = 8
HG_DK = 128
HG_DV = 128
HG_CHUNK = 128
HG_LEVELS = 7

VMEM_LIMIT_BYTES = 56 * 1024 * 1024


def _dot(a, b):
    return jnp.dot(a, b, preferred_element_type=F32)


def _dot_nt(a, b):
    return lax.dot_general(a, b, (((1,), (1,)), ((), ())), preferred_element_type=F32)


def _dot_tn(a, b):
    return lax.dot_general(a, b, (((0,), (0,)), ((), ())), preferred_element_type=F32)


def _rms_scale(x):
    return lax.rsqrt(jnp.mean(x * x, axis=-1, keepdims=True) + NORM_EPS)


def _sigmoid(x):
    return 1.0 / (1.0 + jnp.exp(-x))


ROW_CHUNKS = D_MODEL // 128


def _load_token_rows(ref, n, lead=()):
    return jnp.concatenate(
        [ref[lead + (pl.ds(c, n, stride=ROW_CHUNKS), slice(None))] for c in range(ROW_CHUNKS)], axis=1)


def _store_token_rows(ref, x):
    n = x.shape[0]
    for c in range(ROW_CHUNKS):
        ref[pl.ds(c, n, stride=ROW_CHUNKS), :] = x[:, c * 128:(c + 1) * 128]


def _token_row_slice(ref, token):
    return ref.at[pl.ds(pl.multiple_of(token * ROW_CHUNKS, ROW_CHUNKS), ROW_CHUNKS)]


def _const_spec(shape):
    nd = len(shape)
    return pl.BlockSpec(shape, lambda *_: (0,) * nd, pipeline_mode=pl.Buffered(1))


def _hgrn_decay_sums():
    c = HG_CHUNK
    t = np.arange(c)[:, None]
    u = np.arange(c)[None, :]
    blocks = [(u <= t), (u > t)]
    for l in range(HG_LEVELS):
        m = 1 << l
        mid = (t // (2 * m)) * (2 * m) + m - 1
        upper = (t > mid) & (u > mid) & (u <= t)
        lower = (t <= mid) & (u > t) & (u <= mid)
        blocks.append(upper | lower)
    return np.concatenate(blocks, axis=0).astype(np.float32)


def _hgrn_kernel(x_ref, gain_ref, win_ref, lbp_ref, tsum_ref, gnorm_ref, wout_ref,
                 out_ref, st_ref, o_ref):
    c = HG_CHUNK

    @pl.when(pl.program_id(1) == 0)
    def _():
        st_ref[...] = jnp.zeros_like(st_ref)

    x = x_ref[0]
    hn = (x * _rms_scale(x) * gain_ref[...]).astype(BF16)
    proj = _dot(hn, win_ref[...])
    q = proj[:, 0 * D_MODEL:1 * D_MODEL]
    f_raw = proj[:, 1 * D_MODEL:2 * D_MODEL]
    v = proj[:, 2 * D_MODEL:3 * D_MODEL].astype(BF16)
    o_gate = proj[:, 3 * D_MODEL:4 * D_MODEL]

    lbp = lbp_ref[...]
    lbe = jnp.exp(lbp - jnp.max(lbp, axis=0, keepdims=True))
    lb = lbe[0:1] / jnp.sum(lbe, axis=0, keepdims=True)

    f = lb + (1.0 - lb) * _sigmoid(f_raw)
    g = jnp.log(f)
    k = 1.0 - f

    g_hi = g.astype(BF16)
    g_lo = (g - g_hi.astype(F32)).astype(BF16)
    sums = _dot(tsum_ref[...], jnp.concatenate([g_hi, g_lo], axis=0))
    b = sums[0:c]
    b_last = b[c - 1:c]
    q_bf = q.astype(BF16)
    k_bf = k.astype(BF16)
    qb = (q * jnp.exp(b)).astype(BF16)
    khat = (k * jnp.exp(sums[c:2 * c])).astype(BF16)
    decay = jnp.exp(b_last)
    q_lv, k_lv = [], []
    for l in range(HG_LEVELS):
        w = jnp.exp(sums[(2 + l) * c:(3 + l) * c])
        q_lv.append((q * w).astype(BF16))
        k_lv.append((k * w).astype(BF16))

    ti = lax.broadcasted_iota(jnp.int32, (c, c), 0)
    si = lax.broadcasted_iota(jnp.int32, (c, c), 1)
    diag = ti == si
    xor = ti ^ si
    lv_mask = [((xor >> l) == 1) & (((ti >> l) & 1) == 1) for l in range(HG_LEVELS)]

    for h in range(HG_HEADS):
        hs = slice(h * HG_DK, (h + 1) * HG_DK)
        a = jnp.where(diag, _dot_nt(q_bf[:, hs], k_bf[:, hs]), 0.0)
        for l in range(HG_LEVELS):
            a = jnp.where(lv_mask[l], _dot_nt(q_lv[l][:, hs], k_lv[l][:, hs]), a)
        st = st_ref[h]
        o_h = _dot(a.astype(BF16), v[:, hs]) + _dot_nt(qb[:, hs], st.astype(BF16))
        st_ref[h] = st * decay[:, hs] + _dot_tn(v[:, hs], khat[:, hs])
        o_h = o_h * _rms_scale(o_h) * gnorm_ref[...]
        og = o_gate[:, hs]
        o_ref[:, hs] = (o_h * (og * _sigmoid(og))).astype(BF16)

    out_ref[0] = x + _dot(o_ref[...], wout_ref[...])


def _hgrn_layer(x, gain, w_in, lb_param, gnorm, w_out):
    bsz, seq, d = x.shape
    c = HG_CHUNK
    tsum = _hgrn_decay_sums()
    tsum2 = jnp.asarray(np.concatenate([tsum, tsum], axis=1), dtype=BF16)
    return pl.pallas_call(
        _hgrn_kernel,
        out_shape=jax.ShapeDtypeStruct((bsz, seq, d), F32),
        grid=(bsz, seq // c),
        in_specs=[
            pl.BlockSpec((1, c, d), lambda b, j: (b, j, 0)),
            _const_spec((1, d)),
            _const_spec((d, 4 * d)),
            _const_spec(lb_param.shape),
            _const_spec(tsum2.shape),
            _const_spec((1, HG_DV)),
            _const_spec((d, d)),
        ],
        out_specs=pl.BlockSpec((1, c, d), lambda b, j: (b, j, 0)),
        scratch_shapes=[
            pltpu.VMEM((HG_HEADS, HG_DV, HG_DK), F32),
            pltpu.VMEM((c, d), BF16),
        ],
        compiler_params=pltpu.CompilerParams(
            dimension_semantics=("arbitrary", "arbitrary"),
            vmem_limit_bytes=VMEM_LIMIT_BYTES),
        name="hgrn_layer",
    )(x, gain.reshape(1, d), w_in, lb_param, tsum2, gnorm.reshape(1, HG_DV), w_out)


D_FF_DENSE = 2816
FFN_FF_SPLITS = (0, 768, 1536, 2304, 2816)
FFN_ROWS = 512

SWA_HEAD_DIM = 64
SWA_Q_HEADS = 16
SWA_KV_HEADS = 4
SWA_WINDOW = 128
SWA_KV_LANES = 2 * SWA_HEAD_DIM


def _ffn_kvq_kernel(h_ref, fgain_ref, wg_ref, wu_ref, wd_ref, kvgain_ref, kvw_ref, kgain_ref,
                    qgain_ref, wq_ref, h2_ref, q_ref, k_ref, v_ref):
    h = h_ref[...]
    hn = (h * _rms_scale(h) * fgain_ref[...]).astype(BF16)
    acc = h
    for c in range(len(FFN_FF_SPLITS) - 1):
        sl = slice(FFN_FF_SPLITS[c], FFN_FF_SPLITS[c + 1])
        gate = _dot(hn, wg_ref[:, sl])
        up = _dot(hn, wu_ref[:, sl])
        hid = (gate * _sigmoid(gate) * up).astype(BF16)
        acc = acc + _dot(hid, wd_ref[sl, :])
    h2_ref[...] = acc
    normed = acc * _rms_scale(acc)
    kv = _dot((normed * kvgain_ref[...]).astype(BF16), kvw_ref[...])
    nk = SWA_KV_HEADS * SWA_KV_LANES
    for g in range(SWA_KV_HEADS):
        sl = slice(g * SWA_KV_LANES, (g + 1) * SWA_KV_LANES)
        kg = kv[:, sl]
        k_ref[:, sl] = (kg * _rms_scale(kg) * kgain_ref[...]).astype(BF16)
    v_ref[...] = kv[:, nk:].astype(BF16)
    q_ref[...] = _dot((normed * qgain_ref[...]).astype(BF16), wq_ref[...]).astype(BF16)


def _dup_kv_columns(w):
    d = w.shape[0]
    w = w.reshape(d, SWA_KV_HEADS, 1, SWA_HEAD_DIM)
    return jnp.broadcast_to(w, (d, SWA_KV_HEADS, 2, SWA_HEAD_DIM)).reshape(d, SWA_KV_HEADS * SWA_KV_LANES)


def _ffn_kvq(h1, fgain, wg, wu, wd, kvgain, kv_w, k_gain, qgain, wq):
    t, d = h1.shape
    rows = min(FFN_ROWS, t)
    nkv = SWA_KV_HEADS * SWA_HEAD_DIM
    kvw2 = jnp.concatenate([_dup_kv_columns(kv_w[:, :nkv]), _dup_kv_columns(kv_w[:, nkv:])],
                           axis=1).astype(BF16)
    kgain2 = jnp.concatenate([k_gain, k_gain]).reshape(1, SWA_KV_LANES)
    nk = SWA_KV_HEADS * SWA_KV_LANES
    row_spec = lambda w: pl.BlockSpec((rows, w), lambda i: (i, 0))
    return pl.pallas_call(
        _ffn_kvq_kernel,
        out_shape=(jax.ShapeDtypeStruct((t, d), F32), jax.ShapeDtypeStruct((t, d), BF16),
                   jax.ShapeDtypeStruct((t, nk), BF16), jax.ShapeDtypeStruct((t, nk), BF16)),
        grid=(t // rows,),
        in_specs=[
            row_spec(d),
            _const_spec((1, d)),
            _const_spec((d, D_FF_DENSE)), _const_spec((d, D_FF_DENSE)), _const_spec((D_FF_DENSE, d)),
            _const_spec((1, d)), _const_spec((d, 2 * nk)), _const_spec((1, SWA_KV_LANES)),
            _const_spec((1, d)), _const_spec((d, d)),
        ],
        out_specs=(row_spec(d), row_spec(d), row_spec(nk), row_spec(nk)),
        compiler_params=pltpu.CompilerParams(
            dimension_semantics=("arbitrary",), vmem_limit_bytes=VMEM_LIMIT_BYTES),
        name="ffn_kvq",
    )(h1, fgain.reshape(1, d), wg, wu, wd, kvgain.reshape(1, d), kvw2, kgain2,
      qgain.reshape(1, d), wq)


ATTN_ROWS = 256
REL_BUCKETS = 32
REL_MAX_DIST = 128
N_EXPERTS = 8
ROUTE_LANES = 128
ROUTE_COLS = 8


def _rel_bucket_table():
    w = SWA_WINDOW
    qi = np.arange(w)[:, None]
    kj = np.arange(2 * w)[None, :]
    dist = qi + w - kj
    in_win = (dist >= 0) & (dist < w)
    n = np.maximum(dist, 0)
    max_exact = REL_BUCKETS // 2
    nf = np.maximum(n, 1).astype(np.float64)
    large = max_exact + (np.log(nf / max_exact) / np.log(REL_MAX_DIST / max_exact)
                         * (REL_BUCKETS - max_exact)).astype(np.int64)
    large = np.minimum(large, REL_BUCKETS - 1)
    bucket = np.where(n < max_exact, n, large)
    return np.where(in_win, bucket, -1).astype(np.int32)


def _attn_router_kernel(relb_ref, sink_ref, q_ref, kc_ref, kp_ref, vc_ref, vp_ref, h2_ref,
                        bucket_ref, qgain_ref, wo_ref, fgain_ref, wr_ref,
                        h3_ref, hn2_ref, route_ref, bias_ref, attn_ref):
    w = SWA_WINDOW
    hd = SWA_HEAD_DIM
    rows = q_ref.shape[1]
    first_tile = pl.program_id(1) == 0

    @pl.when(jnp.logical_and(pl.program_id(0) == 0, first_tile))
    def _():
        bucket = bucket_ref[...]
        for h in range(SWA_Q_HEADS):
            plane = jnp.full((w, 2 * w), -jnp.inf, F32)
            for bkt in range(REL_BUCKETS):
                plane = jnp.where(bucket == bkt, relb_ref[bkt * SWA_Q_HEADS + h], plane)
            bias_ref[h] = plane

    lane = lax.broadcasted_iota(jnp.int32, (2 * w, SWA_KV_LANES), 1)
    left = lane < hd
    lane_q = lax.broadcasted_iota(jnp.int32, (w, SWA_KV_LANES), 1)
    left_q = lane_q < hd
    key_idx = lax.broadcasted_iota(jnp.int32, (1, 2 * w), 1)
    qgain = qgain_ref[...]

    k_all = jnp.concatenate([kp_ref[0], kc_ref[0]], axis=0)
    v_all = jnp.concatenate([vp_ref[0], vc_ref[0]], axis=0)
    zero = jnp.zeros((), BF16)
    for blk in range(rows // w):
        r0 = blk * w
        if blk == 0:
            no_prev = jnp.where(jnp.logical_and(first_tile, key_idx < w), -jnp.inf, 0.0)
        else:
            no_prev = None
        for g in range(SWA_KV_HEADS):
            ks = slice(g * SWA_KV_LANES, (g + 1) * SWA_KV_LANES)
            k2 = k_all[r0:r0 + 2 * w, ks]
            v2 = v_all[r0:r0 + 2 * w, ks]
            k_bd = jnp.concatenate([jnp.where(left, k2, zero), jnp.where(left, zero, k2)], axis=0)
            v_bd = jnp.concatenate([jnp.where(left, v2, zero), jnp.where(left, zero, v2)], axis=0)
            pairs = []
            for p in range(2):
                c0 = (4 * g + 2 * p) * hd
                qp = q_ref[0, r0:r0 + w, c0:c0 + 2 * hd].astype(F32)
                sq = qp * qp
                s_l = jnp.sum(jnp.where(left_q, sq, 0.0), axis=-1, keepdims=True)
                s_r = jnp.sum(jnp.where(left_q, 0.0, sq), axis=-1, keepdims=True)
                ms = jnp.where(left_q, s_l, s_r) * (1.0 / hd)
                pairs.append((qp * lax.rsqrt(ms + NORM_EPS) * qgain * (hd ** -0.5)).astype(BF16))
            qs = jnp.concatenate(pairs, axis=0)
            s = _dot_nt(qs, k_bd)
            outs = []
            for p in range(2):
                halves = []
                for side in range(2):
                    hq = 4 * g + 2 * p + side
                    logit = s[p * w:(p + 1) * w, side * 2 * w:(side + 1) * 2 * w] + bias_ref[hq]
                    if no_prev is not None:
                        logit = logit + no_prev
                    sink = sink_ref[hq]
                    m = jnp.maximum(jnp.max(logit, axis=-1, keepdims=True), sink)
                    e = jnp.exp(logit - m)
                    den = jnp.sum(e, axis=-1, keepdims=True) + jnp.exp(sink - m)
                    halves.append((e, den))
                e2 = jnp.concatenate([halves[0][0], halves[1][0]], axis=1).astype(BF16)
                pv = _dot(e2, v_bd)
                pv = pv * jnp.where(left_q, 1.0 / halves[0][1], 1.0 / halves[1][1])
                c0 = (4 * g + 2 * p) * hd
                attn_ref[r0:r0 + w, c0:c0 + 2 * hd] = pv.astype(BF16)

    h3 = h2_ref[0] + _dot(attn_ref[...], wo_ref[...])
    h3_ref[0] = h3
    hn2 = h3 * _rms_scale(h3) * fgain_ref[...]
    _store_token_rows(hn2_ref, hn2)

    hi = hn2.astype(BF16)
    lo = (hn2 - hi.astype(F32)).astype(BF16)
    logits = _dot(hi, wr_ref[0]) + _dot(lo, wr_ref[0]) + _dot(hi, wr_ref[1])
    col = lax.broadcasted_iota(jnp.int32, logits.shape, 1)
    logits = jnp.where(col < N_EXPERTS, logits, -jnp.inf)
    m0 = jnp.max(logits, axis=-1, keepdims=True)
    i0 = jnp.min(jnp.where(logits == m0, col, ROUTE_LANES), axis=-1, keepdims=True)
    rest = jnp.where(col == i0, -jnp.inf, logits)
    m1 = jnp.max(rest, axis=-1, keepdims=True)
    i1 = jnp.min(jnp.where(rest == m1, col, ROUTE_LANES), axis=-1, keepdims=True)
    e1 = jnp.exp(m1 - m0)
    g0 = 1.0 / (1.0 + e1)
    g1 = e1 / (1.0 + e1)
    rc = lax.broadcasted_iota(jnp.int32, (rows, ROUTE_COLS), 1)
    rec = jnp.where(rc == 0, i0.astype(F32),
                    jnp.where(rc == 1, i1.astype(F32),
                              jnp.where(rc == 2, g0, jnp.where(rc == 3, g1, 0.0))))
    route_ref[0] = rec


def _attn_router(q, k2, v2, h2, rel_bias, sink, q_gain, w_o, fgain, w_router):
    bsz, seq, d = h2.shape
    rows = min(ATTN_ROWS, seq)
    w = SWA_WINDOW
    nk = SWA_KV_HEADS * SWA_KV_LANES
    per = rows // w
    bucket = jnp.asarray(_rel_bucket_table())
    qgain2 = jnp.concatenate([q_gain, q_gain]).reshape(1, SWA_KV_LANES)
    wr = jnp.zeros((d, ROUTE_LANES), F32).at[:, :N_EXPERTS].set(w_router)
    wr_hi = wr.astype(BF16)
    wr_lo = (wr - wr_hi.astype(F32)).astype(BF16)
    wr2 = jnp.stack([wr_hi, wr_lo])
    tile = lambda width: pl.BlockSpec((1, rows, width), lambda b, j, *_: (b, j, 0))
    prev = pl.BlockSpec((1, w, nk), lambda b, j, *_: (b, jnp.maximum(j * per - 1, 0), 0))
    const = lambda shape: pl.BlockSpec(shape, lambda b, j, *_: (0,) * len(shape),
                                       pipeline_mode=pl.Buffered(1))
    return pl.pallas_call(
        _attn_router_kernel,
        out_shape=(jax.ShapeDtypeStruct((bsz, seq, d), F32),
                   jax.ShapeDtypeStruct((bsz * seq * ROW_CHUNKS, 128), F32),
                   jax.ShapeDtypeStruct((bsz, seq, ROUTE_COLS), F32)),
        grid_spec=pltpu.PrefetchScalarGridSpec(
            num_scalar_prefetch=2,
            grid=(bsz, seq // rows),
            in_specs=[tile(d), tile(nk), prev, tile(nk), prev, tile(d),
                      const((w, 2 * w)), const((1, SWA_KV_LANES)), const((d, d)), const((1, d)),
                      const((2, d, ROUTE_LANES))],
            out_specs=(tile(d),
                       pl.BlockSpec((rows * ROW_CHUNKS, 128), lambda b, j, *_: (b * (seq // rows) + j, 0)),
                       tile(ROUTE_COLS)),
            scratch_shapes=[pltpu.VMEM((SWA_Q_HEADS, w, 2 * w), F32), pltpu.VMEM((rows, d), BF16)],
        ),
        compiler_params=pltpu.CompilerParams(
            dimension_semantics=("arbitrary", "arbitrary"), vmem_limit_bytes=VMEM_LIMIT_BYTES),
        name="attn_router",
    )(rel_bias.reshape(-1), sink, q, k2, k2, v2, v2, h2, bucket, qgain2, w_o, fgain.reshape(1, d), wr2)


TOP_K = 2
D_FF_EXPERT = 3584
MOE_TILE = 512
MOE_FF_SUB = 512
RANK_BLOCK = 1024
DISPATCH_BLOCK = 1024
COMBINE_BLOCK = 512
ZERO_ROWS = 256
RANK_SUBLANES = 16
DMA_PRIORITIES = 2


def _rank_kernel(e_ref, tri_ref, rank_ref, cnt_ref, carry_ref):
    bt = e_ref.shape[1]

    @pl.when(pl.program_id(0) == 0)
    def _():
        carry_ref[...] = jnp.zeros_like(carry_ref)

    ex = lax.broadcasted_iota(jnp.int32, (RANK_SUBLANES, bt), 0)
    oh0 = ex == e_ref[0:1, :]
    oh1 = ex == e_ref[1:2, :]
    both = oh0.astype(F32) + oh1.astype(F32)
    pos = _dot(both.astype(BF16), tri_ref[...]) + carry_ref[:, 0:1]
    rank_ref[0:1, :] = jnp.sum(jnp.where(oh0, pos, 0.0), axis=0, keepdims=True).astype(jnp.int32)
    rank_ref[1:2, :] = jnp.sum(jnp.where(oh1, pos, 0.0), axis=0, keepdims=True).astype(jnp.int32)
    carry_ref[...] = carry_ref[...] + jnp.sum(both, axis=1, keepdims=True)
    cnt_ref[...] = carry_ref[...]


def _route_ranks(experts):
    _, t = experts.shape
    bt = min(RANK_BLOCK, t)
    tri = jnp.asarray(np.triu(np.ones((bt, bt), np.float32), k=1), dtype=BF16)
    rank, cnt = pl.pallas_call(
        _rank_kernel,
        out_shape=(jax.ShapeDtypeStruct((TOP_K, t), jnp.int32),
                   jax.ShapeDtypeStruct((RANK_SUBLANES, 128), F32)),
        grid=(t // bt,),
        in_specs=[pl.BlockSpec((TOP_K, bt), lambda i: (0, i)), _const_spec((bt, bt))],
        out_specs=(pl.BlockSpec((TOP_K, bt), lambda i: (0, i)),
                   pl.BlockSpec((RANK_SUBLANES, 128), lambda i: (0, 0))),
        scratch_shapes=[pltpu.VMEM((RANK_SUBLANES, 128), F32)],
        compiler_params=pltpu.CompilerParams(dimension_semantics=("arbitrary",)),
        name="route_rank",
    )(experts, tri)
    return rank, cnt[:N_EXPERTS, 0].astype(jnp.int32)


def _row_copy(src, src_token, dst, dst_token, sem):
    return pltpu.make_async_copy(_token_row_slice(src, src_token), _token_row_slice(dst, dst_token), sem)


def _dispatch_kernel(start_ref, zrow_ref, nz_ref, hn_ref, e_ref, rank_ref, xs_ref, dest_ref,
                     dvm_ref, dsm_ref, zero_ref, sem_idx, sem_rows, sem_zero):
    bt = e_ref.shape[1]
    zero_copies = MOE_TILE // ZERO_ROWS

    def zero_copy(z, c):
        row = pl.multiple_of((zrow_ref[z] + c * ZERO_ROWS) * ROW_CHUNKS, ROW_CHUNKS)
        return pltpu.make_async_copy(zero_ref, xs_ref.at[pl.ds(row, ZERO_ROWS * ROW_CHUNKS)], sem_zero)

    @pl.when(pl.program_id(0) == 0)
    def _():
        zero_ref[...] = jnp.zeros_like(zero_ref)

        def start(z, carry):
            for c in range(zero_copies):
                zero_copy(z, c).start()
            return carry

        def wait(z, carry):
            for c in range(zero_copies):
                zero_copy(z, c).wait()
            return carry

        lax.fori_loop(0, nz_ref[0], start, 0)
        lax.fori_loop(0, nz_ref[0], wait, 0)

    e = e_ref[...]
    base = jnp.zeros_like(e)
    for x in range(N_EXPERTS):
        base = jnp.where(e == x, start_ref[x], base)
    dest = base + rank_ref[...]
    dest_ref[...] = dest
    dvm_ref[...] = dest
    idx_copy = pltpu.make_async_copy(dvm_ref, dsm_ref, sem_idx)
    idx_copy.start()
    idx_copy.wait()

    def issue(t, carry):
        for c in range(TOP_K):
            _row_copy(hn_ref, t, xs_ref, dsm_ref[c, t], sem_rows).start(priority=c % DMA_PRIORITIES)
        return carry

    lax.fori_loop(0, bt, issue, 0, unroll=8)
    for c in range(TOP_K):
        pltpu.make_async_copy(hn_ref, xs_ref.at[pl.ds(0, bt * ROW_CHUNKS)], sem_rows).wait()


def _dispatch(hn, experts, rank, start, zrows, nz, m_pad):
    t = hn.shape[0] // ROW_CHUNKS
    bt = min(DISPATCH_BLOCK, t)
    blk = lambda i, *_: (0, i)
    return pl.pallas_call(
        _dispatch_kernel,
        out_shape=(jax.ShapeDtypeStruct((m_pad * ROW_CHUNKS, 128), F32),
                   jax.ShapeDtypeStruct((TOP_K, t), jnp.int32)),
        grid_spec=pltpu.PrefetchScalarGridSpec(
            num_scalar_prefetch=3,
            grid=(t // bt,),
            in_specs=[pl.BlockSpec((bt * ROW_CHUNKS, 128), lambda i, *_: (i, 0)),
                      pl.BlockSpec((TOP_K, bt), blk), pl.BlockSpec((TOP_K, bt), blk)],
            out_specs=(pl.BlockSpec(memory_space=pl.ANY), pl.BlockSpec((TOP_K, bt), blk)),
            scratch_shapes=[pltpu.VMEM((TOP_K, bt), jnp.int32), pltpu.SMEM((TOP_K, bt), jnp.int32),
                            pltpu.VMEM((ZERO_ROWS * ROW_CHUNKS, 128), F32),
                            pltpu.SemaphoreType.DMA, pltpu.SemaphoreType.DMA, pltpu.SemaphoreType.DMA],
        ),
        compiler_params=pltpu.CompilerParams(dimension_semantics=("arbitrary",)),
        name="moe_dispatch",
    )(start, zrows, nz, hn, experts, rank)


def _expert_kernel(te_ref, tv_ref, x_ref, wg_ref, wu_ref, wd_ref, out_ref):
    valid = tv_ref[pl.program_id(0)] == 1

    @pl.when(valid)
    def _():
        xb = _load_token_rows(x_ref, MOE_TILE).astype(BF16)
        acc = jnp.zeros((MOE_TILE, D_MODEL), F32)
        for j in range(D_FF_EXPERT // MOE_FF_SUB):
            sl = slice(j * MOE_FF_SUB, (j + 1) * MOE_FF_SUB)
            gate = _dot(xb, wg_ref[0, :, sl])
            up = _dot(xb, wu_ref[0, :, sl])
            hid = (gate * _sigmoid(gate) * up).astype(BF16)
            acc = acc + _dot(hid, wd_ref[0, sl, :])
        _store_token_rows(out_ref, acc)

    @pl.when(jnp.logical_not(valid))
    def _():
        out_ref[...] = jnp.zeros_like(out_ref)


def _experts(xs, tile_expert, tile_valid, wg, wu, wd):
    d = D_MODEL
    row_block = pl.BlockSpec((MOE_TILE * ROW_CHUNKS, 128), lambda i, te, tv: (i, 0))
    weights = lambda shape: pl.BlockSpec((1,) + shape, lambda i, te, tv: (te[i], 0, 0),
                                         pipeline_mode=pl.Buffered(1))
    return pl.pallas_call(
        _expert_kernel,
        out_shape=jax.ShapeDtypeStruct(xs.shape, F32),
        grid_spec=pltpu.PrefetchScalarGridSpec(
            num_scalar_prefetch=2,
            grid=(xs.shape[0] // (MOE_TILE * ROW_CHUNKS),),
            in_specs=[row_block, weights((d, D_FF_EXPERT)), weights((d, D_FF_EXPERT)),
                      weights((D_FF_EXPERT, d))],
            out_specs=row_block,
        ),
        compiler_params=pltpu.CompilerParams(
            dimension_semantics=("arbitrary",), vmem_limit_bytes=VMEM_LIMIT_BYTES),
        name="moe_experts",
    )(tile_expert, tile_valid, xs, wg, wu, wd)


def _combine_kernel(h_ref, route_ref, dest_ref, yb_ref, out_ref, dsm_ref, ybuf_ref, sem_idx, sem_rows):
    bt = h_ref.shape[0]
    idx_copy = pltpu.make_async_copy(dest_ref, dsm_ref, sem_idx)
    idx_copy.start()
    idx_copy.wait()

    def issue(t, carry):
        for c in range(TOP_K):
            _row_copy(yb_ref, dsm_ref[c, t], ybuf_ref.at[c], t, sem_rows).start(
                priority=c % DMA_PRIORITIES)
        return carry

    lax.fori_loop(0, bt, issue, 0, unroll=8)
    for c in range(TOP_K):
        pltpu.make_async_copy(yb_ref.at[pl.ds(0, bt * ROW_CHUNKS)], ybuf_ref.at[c], sem_rows).wait()
    route = route_ref[...]
    y0 = _load_token_rows(ybuf_ref, bt, (0,))
    y1 = _load_token_rows(ybuf_ref, bt, (1,))
    out_ref[...] = h_ref[...] + route[:, 2:3] * y0 + route[:, 3:4] * y1


def _combine(h3, route, dest, yb):
    t, d = h3.shape
    bt = min(COMBINE_BLOCK, t)
    return pl.pallas_call(
        _combine_kernel,
        out_shape=jax.ShapeDtypeStruct((t, d), F32),
        grid=(t // bt,),
        in_specs=[pl.BlockSpec((bt, d), lambda i: (i, 0)),
                  pl.BlockSpec((bt, ROUTE_COLS), lambda i: (i, 0)),
                  pl.BlockSpec((TOP_K, bt), lambda i: (0, i)),
                  pl.BlockSpec(memory_space=pl.ANY)],
        out_specs=pl.BlockSpec((bt, d), lambda i: (i, 0)),
        scratch_shapes=[pltpu.SMEM((TOP_K, bt), jnp.int32),
                        pltpu.VMEM((TOP_K, bt * ROW_CHUNKS, 128), F32),
                        pltpu.SemaphoreType.DMA, pltpu.SemaphoreType.DMA],
        compiler_params=pltpu.CompilerParams(
            dimension_semantics=("arbitrary",), vmem_limit_bytes=VMEM_LIMIT_BYTES),
        name="moe_combine",
    )(h3, route, dest, yb)


def _moe_layer(h3, hn2, route, wg, wu, wd):
    t, d = h3.shape
    n_tiles = (t * TOP_K) // MOE_TILE + N_EXPERTS
    m_pad = n_tiles * MOE_TILE
    experts = route[:, :TOP_K].astype(jnp.int32).T
    rank, counts = _route_ranks(experts)
    tiles_per = (counts + MOE_TILE - 1) // MOE_TILE
    tile_end = jnp.cumsum(tiles_per)
    start = ((tile_end - tiles_per) * MOE_TILE).astype(jnp.int32)
    tile_ids = jnp.arange(n_tiles, dtype=jnp.int32)
    tile_valid = (tile_ids < tile_end[-1]).astype(jnp.int32)
    last_valid = jnp.maximum(tile_end[-1] - 1, 0)
    tile_expert = jnp.minimum(
        jnp.searchsorted(tile_end, jnp.minimum(tile_ids, last_valid), side="right"),
        N_EXPERTS - 1).astype(jnp.int32)
    has_pad = (counts % MOE_TILE) != 0
    pad_tile = jnp.where(has_pad, tile_end - 1, n_tiles)
    zmask = jnp.zeros((n_tiles + 1,), jnp.int32).at[pad_tile].set(1)[:n_tiles]
    zmask = jnp.maximum(zmask, 1 - tile_valid)
    nz = jnp.sum(zmask).astype(jnp.int32).reshape(1)
    zorder = jnp.argsort(1 - zmask, stable=True).astype(jnp.int32)
    zrows = (zorder[:2 * N_EXPERTS] * MOE_TILE).astype(jnp.int32)
    xs, dest = _dispatch(hn2, experts, rank, start, zrows, nz, m_pad)
    yb = _experts(xs, tile_expert, tile_valid, wg, wu, wd)
    return _combine(h3, route, dest, yb)


def _stages(x, hgrn_w_in, hgrn_lb, hgrn_gnorm, hgrn_w_out, swa_w_q, swa_q_gain, swa_sink, swa_w_o, kv_norm, kv_w, k_gain, rel_bias, attn_norm, ffn_norm, ffn_w_gate, ffn_w_up, ffn_w_down, moe_router, moe_w_gate, moe_w_up, moe_w_down):
    bsz, seq, d = x.shape
    t = bsz * seq
    bf = lambda w: w.astype(BF16)
    h1 = _hgrn_layer(x, attn_norm[0], bf(hgrn_w_in[0]), hgrn_lb, hgrn_gnorm[0], bf(hgrn_w_out[0]))
    h2, q, k2, v2 = _ffn_kvq(h1.reshape(t, d), ffn_norm[0], bf(ffn_w_gate[0]), bf(ffn_w_up[0]),
                             bf(ffn_w_down[0]), kv_norm, kv_w, k_gain, attn_norm[1], bf(swa_w_q[0]))
    nk = SWA_KV_HEADS * SWA_KV_LANES
    h3, hn2, route = _attn_router(q.reshape(bsz, seq, d), k2.reshape(bsz, seq, nk),
                                  v2.reshape(bsz, seq, nk), h2.reshape(bsz, seq, d), rel_bias,
                                  swa_sink[0], swa_q_gain[0], bf(swa_w_o[0]), ffn_norm[1],
                                  moe_router[0])
    h4 = _moe_layer(h3.reshape(t, d), hn2, route.reshape(t, ROUTE_COLS),
                    bf(moe_w_gate[0]), bf(moe_w_up[0]), bf(moe_w_down[0]))
    return {"h1": h1, "h2": h2.reshape(bsz, seq, d), "h3": h3, "h4": h4.reshape(bsz, seq, d)}


def kernel(x, hgrn_w_in, hgrn_lb, hgrn_gnorm, hgrn_w_out, swa_w_q, swa_q_gain, swa_sink, swa_w_o, kv_norm, kv_w, k_gain, rel_bias, attn_norm, ffn_norm, ffn_w_gate, ffn_w_up, ffn_w_down, moe_router, moe_w_gate, moe_w_up, moe_w_down):
    return _stages(x, hgrn_w_in, hgrn_lb, hgrn_gnorm, hgrn_w_out, swa_w_q, swa_q_gain, swa_sink, swa_w_o, kv_norm, kv_w, k_gain, rel_bias, attn_norm, ffn_norm, ffn_w_gate, ffn_w_up, ffn_w_down, moe_router, moe_w_gate, moe_w_up, moe_w_down)["h4"]
```

```python
import functools

import numpy as np
import jax
import jax.numpy as jnp
from jax import lax
from jax.experimental import pallas as pl
from jax.experimental.pallas import tpu as pltpu

F32 = jnp.float32
BF16 = jnp.bfloat16

D_MODEL = 1024
NORM_EPS = 1e-6

HG_HEADS = 8
HG_DK = 128
HG_DV = 128
HG_CHUNK = 128
HG_LEVELS = 7
HG_STEP_CHUNKS = 2

VMEM_LIMIT_BYTES = 56 * 1024 * 1024


def _dot(a, b):
    return jnp.dot(a, b, preferred_element_type=F32)


def _dot_nt(a, b):
    return lax.dot_general(a, b, (((1,), (1,)), ((), ())), preferred_element_type=F32)


def _dot_tn(a, b):
    return lax.dot_general(a, b, (((0,), (0,)), ((), ())), preferred_element_type=F32)


def _rms_scale(x):
    return lax.rsqrt(jnp.mean(x * x, axis=-1, keepdims=True) + NORM_EPS)


def _sigmoid(x):
    return 1.0 / (1.0 + jnp.exp(-x))


ROW_CHUNKS = D_MODEL // 128


def _load_token_rows(ref, n, lead=()):
    return jnp.concatenate(
        [ref[lead + (pl.ds(c, n, stride=ROW_CHUNKS), slice(None))] for c in range(ROW_CHUNKS)], axis=1)


def _store_token_rows(ref, x):
    n = x.shape[0]
    for c in range(ROW_CHUNKS):
        ref[pl.ds(c, n, stride=ROW_CHUNKS), :] = x[:, c * 128:(c + 1) * 128]


def _token_row_slice(ref, token):
    return ref.at[pl.ds(pl.multiple_of(token * ROW_CHUNKS, ROW_CHUNKS), ROW_CHUNKS)]


def _const_spec(shape):
    nd = len(shape)
    return pl.BlockSpec(shape, lambda *_: (0,) * nd, pipeline_mode=pl.Buffered(1))


def _hgrn_decay_sums():
    c = HG_CHUNK
    t = np.arange(c)[:, None]
    u = np.arange(c)[None, :]
    blocks = [(u <= t), (u > t)]
    for l in range(HG_LEVELS):
        m = 1 << l
        mid = (t // (2 * m)) * (2 * m) + m - 1
        upper = (t > mid) & (u > mid) & (u <= t)
        lower = (t <= mid) & (u > t) & (u <= mid)
        blocks.append(upper | lower)
    return np.concatenate(blocks, axis=0).astype(np.float32)


def _hgrn_kernel(x_ref, gain_ref, win_ref, lbp_ref, tsum_ref, gnorm_ref, wout_ref,
                 out_ref, st_ref, o_ref):
    c = HG_CHUNK

    @pl.when(pl.program_id(1) == 0)
    def _():
        st_ref[...] = jnp.zeros_like(st_ref)

    x = x_ref[0]
    hn = (x * _rms_scale(x) * gain_ref[...]).astype(BF16)
    proj = _dot(hn, win_ref[...])

    lbp = lbp_ref[...]
    lbe = jnp.exp(lbp - jnp.max(lbp, axis=0, keepdims=True))
    lb = lbe[0:1] / jnp.sum(lbe, axis=0, keepdims=True)

    ti = lax.broadcasted_iota(jnp.int32, (c, c), 0)
    si = lax.broadcasted_iota(jnp.int32, (c, c), 1)
    diag = ti == si
    xor = ti ^ si
    lv_mask = [((xor >> l) == 1) & (((ti >> l) & 1) == 1) for l in range(HG_LEVELS)]

    heads = [slice(h * HG_DK, (h + 1) * HG_DK) for h in range(HG_HEADS)]
    chunks = [slice(ci * c, (ci + 1) * c) for ci in range(HG_STEP_CHUNKS)]

    ops = []
    for rs in chunks:
        q = proj[rs, 0 * D_MODEL:1 * D_MODEL]
        f = lb + (1.0 - lb) * _sigmoid(proj[rs, 1 * D_MODEL:2 * D_MODEL])
        g = jnp.log(f)
        k = 1.0 - f
        g_hi = g.astype(BF16)
        g_lo = (g - g_hi.astype(F32)).astype(BF16)
        sums = _dot(tsum_ref[...], jnp.concatenate([g_hi, g_lo], axis=0))
        ops.append((q, k, sums))

    staged = []
    for rs, (q, k, sums) in zip(chunks, ops):
        b = sums[0:c]
        q_all = [q.astype(BF16)]
        k_all = [k.astype(BF16)]
        for l in range(HG_LEVELS):
            w = jnp.exp(sums[(2 + l) * c:(3 + l) * c])
            q_all.append((q * w).astype(BF16))
            k_all.append((k * w).astype(BF16))
        staged.append(dict(
            q_all=q_all, k_all=k_all,
            qb=(q * jnp.exp(b)).astype(BF16),
            khat=(k * jnp.exp(sums[c:2 * c])).astype(BF16),
            decay=jnp.exp(b[c - 1:c]),
            v=proj[rs, 2 * D_MODEL:3 * D_MODEL].astype(BF16)))

    masks = [diag] + lv_mask
    scores = []
    for s in staged:
        per_head = []
        for hs in heads:
            a = jnp.zeros((c, c), F32)
            for m, qa, ka in zip(masks, s["q_all"], s["k_all"]):
                a = jnp.where(m, _dot_nt(qa[:, hs], ka[:, hs]), a)
            per_head.append(a.astype(BF16))
        scores.append(per_head)

    outs = []
    for s, per_head in zip(staged, scores):
        o_heads = []
        for h, hs in enumerate(heads):
            st = st_ref[h]
            o_heads.append(_dot(per_head[h], s["v"][:, hs]) + _dot_nt(s["qb"][:, hs], st.astype(BF16)))
            st_ref[h] = st * s["decay"][:, hs] + _dot_tn(s["v"][:, hs], s["khat"][:, hs])
        outs.append(o_heads)

    for rs, o_heads in zip(chunks, outs):
        for h, hs in enumerate(heads):
            o_h = o_heads[h] * _rms_scale(o_heads[h]) * gnorm_ref[...]
            og = proj[rs, 3 * D_MODEL + h * HG_DV:3 * D_MODEL + (h + 1) * HG_DV]
            o_ref[rs, hs] = (o_h * (og * _sigmoid(og))).astype(BF16)

    out_ref[0] = x + _dot(o_ref[...], wout_ref[...])


def _hgrn_layer(x, gain, w_in, lb_param, gnorm, w_out):
    bsz, seq, d = x.shape
    c = HG_CHUNK * HG_STEP_CHUNKS
    tsum = _hgrn_decay_sums()
    tsum2 = jnp.asarray(np.concatenate([tsum, tsum], axis=1), dtype=BF16)
    return pl.pallas_call(
        _hgrn_kernel,
        out_shape=jax.ShapeDtypeStruct((bsz, seq, d), F32),
        grid=(bsz, seq // c),
        in_specs=[
            pl.BlockSpec((1, c, d), lambda b, j: (b, j, 0)),
            _const_spec((1, d)),
            _const_spec((d, 4 * d)),
            _const_spec(lb_param.shape),
            _const_spec(tsum2.shape),
            _const_spec((1, HG_DV)),
            _const_spec((d, d)),
        ],
        out_specs=pl.BlockSpec((1, c, d), lambda b, j: (b, j, 0)),
        scratch_shapes=[
            pltpu.VMEM((HG_HEADS, HG_DV, HG_DK), F32),
            pltpu.VMEM((c, d), BF16),
        ],
        compiler_params=pltpu.CompilerParams(
            dimension_semantics=("arbitrary", "arbitrary"),
            vmem_limit_bytes=VMEM_LIMIT_BYTES),
        name="hgrn_layer",
    )(x, gain.reshape(1, d), w_in, lb_param, tsum2, gnorm.reshape(1, HG_DV), w_out)


D_FF_DENSE = 2816
FFN_FF_SPLITS = (0, 768, 1536, 2304, 2816)
FFN_ROWS = 512

SWA_HEAD_DIM = 64
SWA_Q_HEADS = 16
SWA_KV_HEADS = 4
SWA_WINDOW = 128
SWA_KV_LANES = 2 * SWA_HEAD_DIM


def _ffn_kvq_kernel(h_ref, fgain_ref, wg_ref, wu_ref, wd_ref, kvgain_ref, kvw_ref, kgain_ref,
                    qgain_ref, wq_ref, h2_ref, q_ref, k_ref, v_ref):
    h = h_ref[...]
    hn = (h * _rms_scale(h) * fgain_ref[...]).astype(BF16)
    acc = h
    for c in range(len(FFN_FF_SPLITS) - 1):
        sl = slice(FFN_FF_SPLITS[c], FFN_FF_SPLITS[c + 1])
        gate = _dot(hn, wg_ref[:, sl])
        up = _dot(hn, wu_ref[:, sl])
        hid = (gate * _sigmoid(gate) * up).astype(BF16)
        acc = acc + _dot(hid, wd_ref[sl, :])
    h2_ref[...] = acc
    normed = acc * _rms_scale(acc)
    kv = _dot((normed * kvgain_ref[...]).astype(BF16), kvw_ref[...])
    nk = SWA_KV_HEADS * SWA_KV_LANES
    for g in range(SWA_KV_HEADS):
        sl = slice(g * SWA_KV_LANES, (g + 1) * SWA_KV_LANES)
        kg = kv[:, sl]
        k_ref[:, sl] = (kg * _rms_scale(kg) * kgain_ref[...]).astype(BF16)
    v_ref[...] = kv[:, nk:].astype(BF16)
    q_ref[...] = _dot((normed * qgain_ref[...]).astype(BF16), wq_ref[...]).astype(BF16)


def _dup_kv_columns(w):
    d = w.shape[0]
    w = w.reshape(d, SWA_KV_HEADS, 1, SWA_HEAD_DIM)
    return jnp.broadcast_to(w, (d, SWA_KV_HEADS, 2, SWA_HEAD_DIM)).reshape(d, SWA_KV_HEADS * SWA_KV_LANES)


def _ffn_kvq(h1, fgain, wg, wu, wd, kvgain, kv_w, k_gain, qgain, wq):
    t, d = h1.shape
    rows = min(FFN_ROWS, t)
    nkv = SWA_KV_HEADS * SWA_HEAD_DIM
    kvw2 = jnp.concatenate([_dup_kv_columns(kv_w[:, :nkv]), _dup_kv_columns(kv_w[:, nkv:])],
                           axis=1).astype(BF16)
    kgain2 = jnp.concatenate([k_gain, k_gain]).reshape(1, SWA_KV_LANES)
    nk = SWA_KV_HEADS * SWA_KV_LANES
    row_spec = lambda w: pl.BlockSpec((rows, w), lambda i: (i, 0))
    return pl.pallas_call(
        _ffn_kvq_kernel,
        out_shape=(jax.ShapeDtypeStruct((t, d), F32), jax.ShapeDtypeStruct((t, d), BF16),
                   jax.ShapeDtypeStruct((t, nk), BF16), jax.ShapeDtypeStruct((t, nk), BF16)),
        grid=(t // rows,),
        in_specs=[
            row_spec(d),
            _const_spec((1, d)),
            _const_spec((d, D_FF_DENSE)), _const_spec((d, D_FF_DENSE)), _const_spec((D_FF_DENSE, d)),
            _const_spec((1, d)), _const_spec((d, 2 * nk)), _const_spec((1, SWA_KV_LANES)),
            _const_spec((1, d)), _const_spec((d, d)),
        ],
        out_specs=(row_spec(d), row_spec(d), row_spec(nk), row_spec(nk)),
        compiler_params=pltpu.CompilerParams(
            dimension_semantics=("arbitrary",), vmem_limit_bytes=VMEM_LIMIT_BYTES),
        name="ffn_kvq",
    )(h1, fgain.reshape(1, d), wg, wu, wd, kvgain.reshape(1, d), kvw2, kgain2,
      qgain.reshape(1, d), wq)


ATTN_ROWS = 256
REL_BUCKETS = 32
REL_MAX_DIST = 128
N_EXPERTS = 8
ROUTE_LANES = 128
ROUTE_COLS = 8


def _rel_bucket_table():
    w = SWA_WINDOW
    qi = np.arange(w)[:, None]
    kj = np.arange(2 * w)[None, :]
    dist = qi + w - kj
    in_win = (dist >= 0) & (dist < w)
    n = np.maximum(dist, 0)
    max_exact = REL_BUCKETS // 2
    nf = np.maximum(n, 1).astype(np.float64)
    large = max_exact + (np.log(nf / max_exact) / np.log(REL_MAX_DIST / max_exact)
                         * (REL_BUCKETS - max_exact)).astype(np.int64)
    large = np.minimum(large, REL_BUCKETS - 1)
    bucket = np.where(n < max_exact, n, large)
    return np.where(in_win, bucket, -1).astype(np.int32)


def _attn_router_kernel(relb_ref, sink_ref, q_ref, kc_ref, kp_ref, vc_ref, vp_ref, h2_ref,
                        bucket_ref, qgain_ref, wo_ref, fgain_ref, wr_ref,
                        h3_ref, hn2_ref, route_ref, bias_ref, attn_ref):
    w = SWA_WINDOW
    hd = SWA_HEAD_DIM
    rows = q_ref.shape[1]
    first_tile = pl.program_id(1) == 0

    @pl.when(jnp.logical_and(pl.program_id(0) == 0, first_tile))
    def _():
        bucket = bucket_ref[...]
        for h in range(SWA_Q_HEADS):
            plane = jnp.full((w, 2 * w), -jnp.inf, F32)
            for bkt in range(REL_BUCKETS):
                plane = jnp.where(bucket == bkt, relb_ref[bkt * SWA_Q_HEADS + h], plane)
            bias_ref[h] = plane

    lane = lax.broadcasted_iota(jnp.int32, (2 * w, SWA_KV_LANES), 1)
    left = lane < hd
    lane_q = lax.broadcasted_iota(jnp.int32, (w, SWA_KV_LANES), 1)
    left_q = lane_q < hd
    key_idx = lax.broadcasted_iota(jnp.int32, (1, 2 * w), 1)
    qgain = qgain_ref[...]

    k_all = jnp.concatenate([kp_ref[0], kc_ref[0]], axis=0)
    v_all = jnp.concatenate([vp_ref[0], vc_ref[0]], axis=0)
    zero = jnp.zeros((), BF16)
    no_prev = jnp.where(jnp.logical_and(first_tile, key_idx < w), -jnp.inf, 0.0)
    units = [(blk, g) for blk in range(rows // w) for g in range(SWA_KV_HEADS)]

    score, values = [], []
    for blk, g in units:
        r0 = blk * w
        ks = slice(g * SWA_KV_LANES, (g + 1) * SWA_KV_LANES)
        k2 = k_all[r0:r0 + 2 * w, ks]
        v2 = v_all[r0:r0 + 2 * w, ks]
        k_bd = jnp.concatenate([jnp.where(left, k2, zero), jnp.where(left, zero, k2)], axis=0)
        values.append(jnp.concatenate([jnp.where(left, v2, zero), jnp.where(left, zero, v2)], axis=0))
        pairs = []
        for p in range(2):
            c0 = (4 * g + 2 * p) * hd
            qp = q_ref[0, r0:r0 + w, c0:c0 + 2 * hd].astype(F32)
            sq = qp * qp
            s_l = jnp.sum(jnp.where(left_q, sq, 0.0), axis=-1, keepdims=True)
            s_r = jnp.sum(jnp.where(left_q, 0.0, sq), axis=-1, keepdims=True)
            ms = jnp.where(left_q, s_l, s_r) * (1.0 / hd)
            pairs.append((qp * lax.rsqrt(ms + NORM_EPS) * qgain * (hd ** -0.5)).astype(BF16))
        qs = jnp.concatenate(pairs, axis=0)
        score.append(_dot_nt(qs, k_bd))

    probs, scales = [], []
    for (blk, g), s in zip(units, score):
        e_rows, inv_rows = [], []
        for p in range(2):
            halves = []
            for side in range(2):
                hq = 4 * g + 2 * p + side
                logit = s[p * w:(p + 1) * w, side * 2 * w:(side + 1) * 2 * w] + bias_ref[hq]
                if blk == 0:
                    logit = logit + no_prev
                sink = sink_ref[hq]
                m = jnp.maximum(jnp.max(logit, axis=-1, keepdims=True), sink)
                e = jnp.exp(logit - m)
                den = jnp.sum(e, axis=-1, keepdims=True) + jnp.exp(sink - m)
                halves.append((e, den))
            e_rows.append(jnp.concatenate([halves[0][0], halves[1][0]], axis=1).astype(BF16))
            inv_rows.append(jnp.where(left_q, 1.0 / halves[0][1], 1.0 / halves[1][1]))
        probs.append(jnp.concatenate(e_rows, axis=0))
        scales.append(jnp.concatenate(inv_rows, axis=0))

    for (blk, g), e2, v_bd, inv in zip(units, probs, values, scales):
        pv = _dot(e2, v_bd) * inv
        r0 = blk * w
        for p in range(2):
            c0 = (4 * g + 2 * p) * hd
            attn_ref[r0:r0 + w, c0:c0 + 2 * hd] = pv[p * w:(p + 1) * w].astype(BF16)

    h3 = h2_ref[0] + _dot(attn_ref[...], wo_ref[...])
    h3_ref[0] = h3
    hn2 = h3 * _rms_scale(h3) * fgain_ref[...]
    _store_token_rows(hn2_ref, hn2)

    hi = hn2.astype(BF16)
    lo = (hn2 - hi.astype(F32)).astype(BF16)
    logits = _dot(hi, wr_ref[0]) + _dot(lo, wr_ref[0]) + _dot(hi, wr_ref[1])
    col = lax.broadcasted_iota(jnp.int32, logits.shape, 1)
    logits = jnp.where(col < N_EXPERTS, logits, -jnp.inf)
    m0 = jnp.max(logits, axis=-1, keepdims=True)
    i0 = jnp.min(jnp.where(logits == m0, col, ROUTE_LANES), axis=-1, keepdims=True)
    rest = jnp.where(col == i0, -jnp.inf, logits)
    m1 = jnp.max(rest, axis=-1, keepdims=True)
    i1 = jnp.min(jnp.where(rest == m1, col, ROUTE_LANES), axis=-1, keepdims=True)
    e1 = jnp.exp(m1 - m0)
    g0 = 1.0 / (1.0 + e1)
    g1 = e1 / (1.0 + e1)
    rc = lax.broadcasted_iota(jnp.int32, (rows, ROUTE_COLS), 1)
    rec = jnp.where(rc == 0, i0.astype(F32),
                    jnp.where(rc == 1, i1.astype(F32),
                              jnp.where(rc == 2, g0, jnp.where(rc == 3, g1, 0.0))))
    route_ref[0] = rec


def _attn_router(q, k2, v2, h2, rel_bias, sink, q_gain, w_o, fgain, w_router):
    bsz, seq, d = h2.shape
    rows = min(ATTN_ROWS, seq)
    w = SWA_WINDOW
    nk = SWA_KV_HEADS * SWA_KV_LANES
    per = rows // w
    bucket = jnp.asarray(_rel_bucket_table())
    qgain2 = jnp.concatenate([q_gain, q_gain]).reshape(1, SWA_KV_LANES)
    wr = jnp.zeros((d, ROUTE_LANES), F32).at[:, :N_EXPERTS].set(w_router)
    wr_hi = wr.astype(BF16)
    wr_lo = (wr - wr_hi.astype(F32)).astype(BF16)
    wr2 = jnp.stack([wr_hi, wr_lo])
    tile = lambda width: pl.BlockSpec((1, rows, width), lambda b, j, *_: (b, j, 0))
    prev = pl.BlockSpec((1, w, nk), lambda b, j, *_: (b, jnp.maximum(j * per - 1, 0), 0))
    const = lambda shape: pl.BlockSpec(shape, lambda b, j, *_: (0,) * len(shape),
                                       pipeline_mode=pl.Buffered(1))
    return pl.pallas_call(
        _attn_router_kernel,
        out_shape=(jax.ShapeDtypeStruct((bsz, seq, d), F32),
                   jax.ShapeDtypeStruct((bsz * seq * ROW_CHUNKS, 128), F32),
                   jax.ShapeDtypeStruct((bsz, seq, ROUTE_COLS), F32)),
        grid_spec=pltpu.PrefetchScalarGridSpec(
            num_scalar_prefetch=2,
            grid=(bsz, seq // rows),
            in_specs=[tile(d), tile(nk), prev, tile(nk), prev, tile(d),
                      const((w, 2 * w)), const((1, SWA_KV_LANES)), const((d, d)), const((1, d)),
                      const((2, d, ROUTE_LANES))],
            out_specs=(tile(d),
                       pl.BlockSpec((rows * ROW_CHUNKS, 128), lambda b, j, *_: (b * (seq // rows) + j, 0)),
                       tile(ROUTE_COLS)),
            scratch_shapes=[pltpu.VMEM((SWA_Q_HEADS, w, 2 * w), F32), pltpu.VMEM((rows, d), BF16)],
        ),
        compiler_params=pltpu.CompilerParams(
            dimension_semantics=("arbitrary", "arbitrary"), vmem_limit_bytes=VMEM_LIMIT_BYTES),
        name="attn_router",
    )(rel_bias.reshape(-1), sink, q, k2, k2, v2, v2, h2, bucket, qgain2, w_o, fgain.reshape(1, d), wr2)


TOP_K = 2
D_FF_EXPERT = 3584
MOE_TILE = 512
MOE_FF_SUB = 512
RANK_BLOCK = 1024
DISPATCH_BLOCK = 1024
COMBINE_BLOCK = 512
ZERO_ROWS = 256
RANK_SUBLANES = 16
DMA_PRIORITIES = 2


def _rank_kernel(e_ref, tri_ref, rank_ref, cnt_ref, carry_ref):
    bt = e_ref.shape[1]

    @pl.when(pl.program_id(0) == 0)
    def _():
        carry_ref[...] = jnp.zeros_like(carry_ref)

    ex = lax.broadcasted_iota(jnp.int32, (RANK_SUBLANES, bt), 0)
    oh0 = ex == e_ref[0:1, :]
    oh1 = ex == e_ref[1:2, :]
    both = oh0.astype(F32) + oh1.astype(F32)
    pos = _dot(both.astype(BF16), tri_ref[...]) + carry_ref[:, 0:1]
    rank_ref[0:1, :] = jnp.sum(jnp.where(oh0, pos, 0.0), axis=0, keepdims=True).astype(jnp.int32)
    rank_ref[1:2, :] = jnp.sum(jnp.where(oh1, pos, 0.0), axis=0, keepdims=True).astype(jnp.int32)
    carry_ref[...] = carry_ref[...] + jnp.sum(both, axis=1, keepdims=True)
    cnt_ref[...] = carry_ref[...]


def _route_ranks(experts):
    _, t = experts.shape
    bt = min(RANK_BLOCK, t)
    tri = jnp.asarray(np.triu(np.ones((bt, bt), np.float32), k=1), dtype=BF16)
    rank, cnt = pl.pallas_call(
        _rank_kernel,
        out_shape=(jax.ShapeDtypeStruct((TOP_K, t), jnp.int32),
                   jax.ShapeDtypeStruct((RANK_SUBLANES, 128), F32)),
        grid=(t // bt,),
        in_specs=[pl.BlockSpec((TOP_K, bt), lambda i: (0, i)), _const_spec((bt, bt))],
        out_specs=(pl.BlockSpec((TOP_K, bt), lambda i: (0, i)),
                   pl.BlockSpec((RANK_SUBLANES, 128), lambda i: (0, 0))),
        scratch_shapes=[pltpu.VMEM((RANK_SUBLANES, 128), F32)],
        compiler_params=pltpu.CompilerParams(dimension_semantics=("arbitrary",)),
        name="route_rank",
    )(experts, tri)
    return rank, cnt[:N_EXPERTS, 0].astype(jnp.int32)


def _row_copy(src, src_token, dst, dst_token, sem):
    return pltpu.make_async_copy(_token_row_slice(src, src_token), _token_row_slice(dst, dst_token), sem)


def _dispatch_kernel(start_ref, zrow_ref, nz_ref, hn_ref, e_ref, rank_ref, xs_ref, dest_ref,
                     dvm_ref, dsm_ref, zero_ref, sem_idx, sem_rows, sem_zero):
    bt = e_ref.shape[1]
    zero_copies = MOE_TILE // ZERO_ROWS

    def zero_copy(z, c):
        row = pl.multiple_of((zrow_ref[z] + c * ZERO_ROWS) * ROW_CHUNKS, ROW_CHUNKS)
        return pltpu.make_async_copy(zero_ref, xs_ref.at[pl.ds(row, ZERO_ROWS * ROW_CHUNKS)], sem_zero)

    @pl.when(pl.program_id(0) == 0)
    def _():
        zero_ref[...] = jnp.zeros_like(zero_ref)

        def start(z, carry):
            for c in range(zero_copies):
                zero_copy(z, c).start()
            return carry

        def wait(z, carry):
            for c in range(zero_copies):
                zero_copy(z, c).wait()
            return carry

        lax.fori_loop(0, nz_ref[0], start, 0)
        lax.fori_loop(0, nz_ref[0], wait, 0)

    e = e_ref[...]
    base = jnp.zeros_like(e)
    for x in range(N_EXPERTS):
        base = jnp.where(e == x, start_ref[x], base)
    dest = base + rank_ref[...]
    dest_ref[...] = dest
    dvm_ref[...] = dest
    idx_copy = pltpu.make_async_copy(dvm_ref, dsm_ref, sem_idx)
    idx_copy.start()
    idx_copy.wait()

    def issue(t, carry):
        for c in range(TOP_K):
            _row_copy(hn_ref, t, xs_ref, dsm_ref[c, t], sem_rows).start(priority=c % DMA_PRIORITIES)
        return carry

    lax.fori_loop(0, bt, issue, 0, unroll=8)
    for c in range(TOP_K):
        pltpu.make_async_copy(hn_ref, xs_ref.at[pl.ds(0, bt * ROW_CHUNKS)], sem_rows).wait()


def _dispatch(hn, experts, rank, start, zrows, nz, m_pad):
    t = hn.shape[0] // ROW_CHUNKS
    bt = min(DISPATCH_BLOCK, t)
    blk = lambda i, *_: (0, i)
    return pl.pallas_call(
        _dispatch_kernel,
        out_shape=(jax.ShapeDtypeStruct((m_pad * ROW_CHUNKS, 128), F32),
                   jax.ShapeDtypeStruct((TOP_K, t), jnp.int32)),
        grid_spec=pltpu.PrefetchScalarGridSpec(
            num_scalar_prefetch=3,
            grid=(t // bt,),
            in_specs=[pl.BlockSpec((bt * ROW_CHUNKS, 128), lambda i, *_: (i, 0)),
                      pl.BlockSpec((TOP_K, bt), blk), pl.BlockSpec((TOP_K, bt), blk)],
            out_specs=(pl.BlockSpec(memory_space=pl.ANY), pl.BlockSpec((TOP_K, bt), blk)),
            scratch_shapes=[pltpu.VMEM((TOP_K, bt), jnp.int32), pltpu.SMEM((TOP_K, bt), jnp.int32),
                            pltpu.VMEM((ZERO_ROWS * ROW_CHUNKS, 128), F32),
                            pltpu.SemaphoreType.DMA, pltpu.SemaphoreType.DMA, pltpu.SemaphoreType.DMA],
        ),
        compiler_params=pltpu.CompilerParams(dimension_semantics=("arbitrary",)),
        name="moe_dispatch",
    )(start, zrows, nz, hn, experts, rank)


def _expert_kernel(te_ref, tv_ref, x_ref, wg_ref, wu_ref, wd_ref, out_ref):
    valid = tv_ref[pl.program_id(0)] == 1

    @pl.when(valid)
    def _():
        xb = _load_token_rows(x_ref, MOE_TILE).astype(BF16)
        acc = jnp.zeros((MOE_TILE, D_MODEL), F32)
        for j in range(D_FF_EXPERT // MOE_FF_SUB):
            sl = slice(j * MOE_FF_SUB, (j + 1) * MOE_FF_SUB)
            gate = _dot(xb, wg_ref[0, :, sl])
            up = _dot(xb, wu_ref[0, :, sl])
            hid = (gate * _sigmoid(gate) * up).astype(BF16)
            acc = acc + _dot(hid, wd_ref[0, sl, :])
        _store_token_rows(out_ref, acc)

    @pl.when(jnp.logical_not(valid))
    def _():
        out_ref[...] = jnp.zeros_like(out_ref)


def _experts(xs, tile_expert, tile_valid, wg, wu, wd):
    d = D_MODEL
    row_block = pl.BlockSpec((MOE_TILE * ROW_CHUNKS, 128), lambda i, te, tv: (i, 0))
    weights = lambda shape: pl.BlockSpec((1,) + shape, lambda i, te, tv: (te[i], 0, 0),
                                         pipeline_mode=pl.Buffered(1))
    return pl.pallas_call(
        _expert_kernel,
        out_shape=jax.ShapeDtypeStruct(xs.shape, F32),
        grid_spec=pltpu.PrefetchScalarGridSpec(
            num_scalar_prefetch=2,
            grid=(xs.shape[0] // (MOE_TILE * ROW_CHUNKS),),
            in_specs=[row_block, weights((d, D_FF_EXPERT)), weights((d, D_FF_EXPERT)),
                      weights((D_FF_EXPERT, d))],
            out_specs=row_block,
        ),
        compiler_params=pltpu.CompilerParams(
            dimension_semantics=("arbitrary",), vmem_limit_bytes=VMEM_LIMIT_BYTES),
        name="moe_experts",
    )(tile_expert, tile_valid, xs, wg, wu, wd)


def _combine_kernel(h_ref, route_ref, dest_ref, yb_ref, out_ref, dsm_ref, ybuf_ref, sem_idx, sem_rows):
    bt = h_ref.shape[0]
    idx_copy = pltpu.make_async_copy(dest_ref, dsm_ref, sem_idx)
    idx_copy.start()
    idx_copy.wait()

    def issue(t, carry):
        for c in range(TOP_K):
            _row_copy(yb_ref, dsm_ref[c, t], ybuf_ref.at[c], t, sem_rows).start(
                priority=c % DMA_PRIORITIES)
        return carry

    lax.fori_loop(0, bt, issue, 0, unroll=8)
    for c in range(TOP_K):
        pltpu.make_async_copy(yb_ref.at[pl.ds(0, bt * ROW_CHUNKS)], ybuf_ref.at[c], sem_rows).wait()
    route = route_ref[...]
    y0 = _load_token_rows(ybuf_ref, bt, (0,))
    y1 = _load_token_rows(ybuf_ref, bt, (1,))
    out_ref[...] = h_ref[...] + route[:, 2:3] * y0 + route[:, 3:4] * y1


def _combine(h3, route, dest, yb):
    t, d = h3.shape
    bt = min(COMBINE_BLOCK, t)
    return pl.pallas_call(
        _combine_kernel,
        out_shape=jax.ShapeDtypeStruct((t, d), F32),
        grid=(t // bt,),
        in_specs=[pl.BlockSpec((bt, d), lambda i: (i, 0)),
                  pl.BlockSpec((bt, ROUTE_COLS), lambda i: (i, 0)),
                  pl.BlockSpec((TOP_K, bt), lambda i: (0, i)),
                  pl.BlockSpec(memory_space=pl.ANY)],
        out_specs=pl.BlockSpec((bt, d), lambda i: (i, 0)),
        scratch_shapes=[pltpu.SMEM((TOP_K, bt), jnp.int32),
                        pltpu.VMEM((TOP_K, bt * ROW_CHUNKS, 128), F32),
                        pltpu.SemaphoreType.DMA, pltpu.SemaphoreType.DMA],
        compiler_params=pltpu.CompilerParams(
            dimension_semantics=("arbitrary",), vmem_limit_bytes=VMEM_LIMIT_BYTES),
        name="moe_combine",
    )(h3, route, dest, yb)


def _moe_layer(h3, hn2, route, wg, wu, wd):
    t, d = h3.shape
    n_tiles = (t * TOP_K) // MOE_TILE + N_EXPERTS
    m_pad = n_tiles * MOE_TILE
    experts = route[:, :TOP_K].astype(jnp.int32).T
    rank, counts = _route_ranks(experts)
    tiles_per = (counts + MOE_TILE - 1) // MOE_TILE
    tile_end = jnp.cumsum(tiles_per)
    start = ((tile_end - tiles_per) * MOE_TILE).astype(jnp.int32)
    tile_ids = jnp.arange(n_tiles, dtype=jnp.int32)
    tile_valid = (tile_ids < tile_end[-1]).astype(jnp.int32)
    last_valid = jnp.maximum(tile_end[-1] - 1, 0)
    tile_expert = jnp.minimum(
        jnp.searchsorted(tile_end, jnp.minimum(tile_ids, last_valid), side="right"),
        N_EXPERTS - 1).astype(jnp.int32)
    has_pad = (counts % MOE_TILE) != 0
    pad_tile = jnp.where(has_pad, tile_end - 1, n_tiles)
    zmask = jnp.zeros((n_tiles + 1,), jnp.int32).at[pad_tile].set(1)[:n_tiles]
    zmask = jnp.maximum(zmask, 1 - tile_valid)
    nz = jnp.sum(zmask).astype(jnp.int32).reshape(1)
    zorder = jnp.argsort(1 - zmask, stable=True).astype(jnp.int32)
    zrows = (zorder[:2 * N_EXPERTS] * MOE_TILE).astype(jnp.int32)
    xs, dest = _dispatch(hn2, experts, rank, start, zrows, nz, m_pad)
    yb = _experts(xs, tile_expert, tile_valid, wg, wu, wd)
    return _combine(h3, route, dest, yb)


def _stages(x, hgrn_w_in, hgrn_lb, hgrn_gnorm, hgrn_w_out, swa_w_q, swa_q_gain, swa_sink, swa_w_o, kv_norm, kv_w, k_gain, rel_bias, attn_norm, ffn_norm, ffn_w_gate, ffn_w_up, ffn_w_down, moe_router, moe_w_gate, moe_w_up, moe_w_down):
    bsz, seq, d = x.shape
    t = bsz * seq
    bf = lambda w: w.astype(BF16)
    h1 = _hgrn_layer(x, attn_norm[0], bf(hgrn_w_in[0]), hgrn_lb, hgrn_gnorm[0], bf(hgrn_w_out[0]))
    h2, q, k2, v2 = _ffn_kvq(h1.reshape(t, d), ffn_norm[0], bf(ffn_w_gate[0]), bf(ffn_w_up[0]),
                             bf(ffn_w_down[0]), kv_norm, kv_w, k_gain, attn_norm[1], bf(swa_w_q[0]))
    nk = SWA_KV_HEADS * SWA_KV_LANES
    h3, hn2, route = _attn_router(q.reshape(bsz, seq, d), k2.reshape(bsz, seq, nk),
                                  v2.reshape(bsz, seq, nk), h2.reshape(bsz, seq, d), rel_bias,
                                  swa_sink[0], swa_q_gain[0], bf(swa_w_o[0]), ffn_norm[1],
                                  moe_router[0])
    h4 = _moe_layer(h3.reshape(t, d), hn2, route.reshape(t, ROUTE_COLS),
                    bf(moe_w_gate[0]), bf(moe_w_up[0]), bf(moe_w_down[0]))
    return {"h1": h1, "h2": h2.reshape(bsz, seq, d), "h3": h3, "h4": h4.reshape(bsz, seq, d)}


def kernel(x, hgrn_w_in, hgrn_lb, hgrn_gnorm, hgrn_w_out, swa_w_q, swa_q_gain, swa_sink, swa_w_o, kv_norm, kv_w, k_gain, rel_bias, attn_norm, ffn_norm, ffn_w_gate, ffn_w_up, ffn_w_down, moe_router, moe_w_gate, moe_w_up, moe_w_down):
    return _stages(x, hgrn_w_in, hgrn_lb, hgrn_gnorm, hgrn_w_out, swa_w_q, swa_q_gain, swa_sink, swa_w_o, kv_norm, kv_w, k_gain, rel_bias, attn_norm, ffn_norm, ffn_w_gate, ffn_w_up, ffn_w_down, moe_router, moe_w_gate, moe_w_up, moe_w_down)["h4"]
```

```python
import functools

import numpy as np
import jax
import jax.numpy as jnp
from jax import lax
from jax.experimental import pallas as pl
from jax.experimental.pallas import tpu as pltpu

F32 = jnp.float32
BF16 = jnp.bfloat16

D_MODEL = 1024
NORM_EPS = 1e-6
LOG2_E = 1.4426950408889634

HG_HEADS = 8
HG_DK = 128
HG_DV = 128
HG_CHUNK = 128
HG_LEVELS = 7
HG_STEP_CHUNKS = 2

VMEM_LIMIT_BYTES = 56 * 1024 * 1024


def _dot(a, b):
    return jnp.dot(a, b, preferred_element_type=F32)


def _dot_nt(a, b):
    return lax.dot_general(a, b, (((1,), (1,)), ((), ())), preferred_element_type=F32)


def _dot_tn(a, b):
    return lax.dot_general(a, b, (((0,), (0,)), ((), ())), preferred_element_type=F32)


def _rms_scale(x):
    return lax.rsqrt(jnp.mean(x * x, axis=-1, keepdims=True) + NORM_EPS)


def _sigmoid(x):
    return 1.0 / (1.0 + jnp.exp(-x))


ROW_CHUNKS = D_MODEL // 128


def _load_token_rows(ref, n, lead=()):
    return jnp.concatenate(
        [ref[lead + (pl.ds(c, n, stride=ROW_CHUNKS), slice(None))] for c in range(ROW_CHUNKS)], axis=1)


def _store_token_rows(ref, x):
    n = x.shape[0]
    for c in range(ROW_CHUNKS):
        ref[pl.ds(c, n, stride=ROW_CHUNKS), :] = x[:, c * 128:(c + 1) * 128]


def _token_row_slice(ref, token):
    return ref.at[pl.ds(pl.multiple_of(token * ROW_CHUNKS, ROW_CHUNKS), ROW_CHUNKS)]


def _const_spec(shape):
    nd = len(shape)
    return pl.BlockSpec(shape, lambda *_: (0,) * nd, pipeline_mode=pl.Buffered(1))


def _hgrn_level_log_decay(b, level):
    c = HG_CHUNK
    m = 1 << level
    parts = []
    if 2 * m >= 8:
        for start in range(0, c, 2 * m):
            mid = start + m - 1
            parts.append(b[start:start + 2 * m] - b[mid:mid + 1])
    else:
        first_block = lax.broadcasted_iota(jnp.int32, (8, b.shape[1]), 0) < 4
        for start in range(0, c, 8):
            b_mid = jnp.where(first_block, b[start + 1:start + 2], b[start + 5:start + 6])
            parts.append(b[start:start + 8] - b_mid)
    return jnp.concatenate(parts, axis=0)


def _hgrn_kernel(x_ref, gain_ref, win_ref, lbp_ref, tsum_ref, gnorm_ref, wout_ref,
                 out_ref, st_ref, o_ref):
    c = HG_CHUNK

    @pl.when(pl.program_id(1) == 0)
    def _():
        st_ref[...] = jnp.zeros_like(st_ref)

    chunks = [slice(ci * c, (ci + 1) * c) for ci in range(HG_STEP_CHUNKS)]

    lbp = lbp_ref[...]
    lbe = jnp.exp(lbp - jnp.max(lbp, axis=0, keepdims=True))
    lb = lbe[0:1] / jnp.sum(lbe, axis=0, keepdims=True)

    x = x_ref[0]
    hn = (x * _rms_scale(x) * gain_ref[...]).astype(BF16)
    projs = [_dot(hn[rs], win_ref[...]) for rs in chunks]
    gated = []
    for proj in projs:
        f = lb + (1.0 - lb) * _sigmoid(proj[:, 1 * D_MODEL:2 * D_MODEL])
        g = jnp.log(f)
        g_hi = g.astype(BF16)
        g_lo = (g - g_hi.astype(F32)).astype(BF16)
        gated.append((f, _dot(tsum_ref[...], jnp.concatenate([g_hi, g_lo], axis=0))))

    ti = lax.broadcasted_iota(jnp.int32, (c, c), 0)
    si = lax.broadcasted_iota(jnp.int32, (c, c), 1)
    diag = ti == si
    xor = ti ^ si
    lv_mask = [((xor >> l) == 1) & (((ti >> l) & 1) == 1) for l in range(HG_LEVELS)]

    heads = [slice(h * HG_DK, (h + 1) * HG_DK) for h in range(HG_HEADS)]

    row = lax.broadcasted_iota(jnp.int32, (c, HG_DK), 0)
    upper_rows = [((row >> l) & 1) == 1 for l in range(HG_LEVELS)]
    signs = [jnp.where(u, LOG2_E, -LOG2_E) for u in upper_rows]

    staged = []
    for proj, (f, b) in zip(projs, gated):
        per_chunk = []
        for h, hs in enumerate(heads):
            q_h = proj[:, hs]
            f_h = f[:, hs]
            b_h = b[:, hs]
            k_h = 1.0 - f_h
            b_last = b_h[c - 1:c]
            per_level = []
            for l in range(HG_LEVELS):
                if l == 0:
                    w = jnp.where(upper_rows[0], f_h, 1.0)
                else:
                    w = jnp.exp2(_hgrn_level_log_decay(b_h, l) * signs[l])
                per_level.append((jnp.where(upper_rows[l], q_h, k_h) * w).astype(BF16))
            per_chunk.append(dict(
                levels=per_level, q=q_h.astype(BF16), k=k_h.astype(BF16),
                qb=(q_h * jnp.exp(b_h)).astype(BF16),
                khat=(k_h * jnp.exp(b_last - b_h)).astype(BF16),
                decay=jnp.exp(b_last),
                v=proj[:, 2 * D_MODEL + h * HG_DV:2 * D_MODEL + (h + 1) * HG_DV].astype(BF16),
                o_gate=proj[:, 3 * D_MODEL + h * HG_DV:3 * D_MODEL + (h + 1) * HG_DV]))
        staged.append(per_chunk)

    scores = []
    for per_chunk in staged:
        per_head = []
        for s in per_chunk:
            a = jnp.where(diag, _dot_nt(s["q"], s["k"]), 0.0)
            for m, lv in zip(lv_mask, s["levels"]):
                a = jnp.where(m, _dot_nt(lv, lv), a)
            per_head.append(a.astype(BF16))
        scores.append(per_head)

    outs = []
    for per_chunk, per_head in zip(staged, scores):
        o_heads = []
        for h, s in enumerate(per_chunk):
            st = st_ref[h]
            o_heads.append(_dot(per_head[h], s["v"]) + _dot_nt(s["qb"], st.astype(BF16)))
            st_ref[h] = st * s["decay"] + _dot_tn(s["v"], s["khat"])
        outs.append(o_heads)

    for rs, per_chunk, o_heads in zip(chunks, staged, outs):
        for h, hs in enumerate(heads):
            o_h = o_heads[h] * _rms_scale(o_heads[h]) * gnorm_ref[...]
            og = per_chunk[h]["o_gate"]
            o_ref[rs, hs] = (o_h * (og * _sigmoid(og))).astype(BF16)

    out_ref[0] = x + _dot(o_ref[...], wout_ref[...])


def _hgrn_layer(x, gain, w_in, lb_param, gnorm, w_out):
    bsz, seq, d = x.shape
    c = HG_CHUNK * HG_STEP_CHUNKS
    tril = np.tril(np.ones((HG_CHUNK, HG_CHUNK), np.float32))
    tsum2 = jnp.asarray(np.concatenate([tril, tril], axis=1), dtype=BF16)
    return pl.pallas_call(
        _hgrn_kernel,
        out_shape=jax.ShapeDtypeStruct((bsz, seq, d), F32),
        grid=(bsz, seq // c),
        in_specs=[
            pl.BlockSpec((1, c, d), lambda b, j: (b, j, 0)),
            _const_spec((1, d)),
            _const_spec((d, 4 * d)),
            _const_spec(lb_param.shape),
            _const_spec(tsum2.shape),
            _const_spec((1, HG_DV)),
            _const_spec((d, d)),
        ],
        out_specs=pl.BlockSpec((1, c, d), lambda b, j: (b, j, 0)),
        scratch_shapes=[
            pltpu.VMEM((HG_HEADS, HG_DV, HG_DK), F32),
            pltpu.VMEM((c, d), BF16),
        ],
        compiler_params=pltpu.CompilerParams(
            dimension_semantics=("arbitrary", "arbitrary"),
            vmem_limit_bytes=VMEM_LIMIT_BYTES),
        name="hgrn_layer",
    )(x, gain.reshape(1, d), w_in, lb_param, tsum2, gnorm.reshape(1, HG_DV), w_out)


D_FF_DENSE = 2816
FFN_FF_SPLITS = (0, 768, 1536, 2304, 2816)
FFN_ROWS = 512

SWA_HEAD_DIM = 64
SWA_Q_HEADS = 16
SWA_KV_HEADS = 4
SWA_WINDOW = 128
SWA_KV_LANES = 2 * SWA_HEAD_DIM


def _ffn_kvq_kernel(h_ref, fgain_ref, wg_ref, wu_ref, wd_ref, kvgain_ref, kvw_ref, kgain_ref,
                    qgain_ref, wq_ref, h2_ref, q_ref, k_ref, v_ref):
    h = h_ref[...]
    hn = (h * _rms_scale(h) * fgain_ref[...]).astype(BF16)
    acc = h
    for c in range(len(FFN_FF_SPLITS) - 1):
        sl = slice(FFN_FF_SPLITS[c], FFN_FF_SPLITS[c + 1])
        gate = _dot(hn, wg_ref[:, sl])
        up = _dot(hn, wu_ref[:, sl])
        hid = (gate * _sigmoid(gate) * up).astype(BF16)
        acc = acc + _dot(hid, wd_ref[sl, :])
    h2_ref[...] = acc
    normed = acc * _rms_scale(acc)
    kv = _dot((normed * kvgain_ref[...]).astype(BF16), kvw_ref[...])
    nk = SWA_KV_HEADS * SWA_KV_LANES
    for g in range(SWA_KV_HEADS):
        sl = slice(g * SWA_KV_LANES, (g + 1) * SWA_KV_LANES)
        kg = kv[:, sl]
        k_ref[:, sl] = (kg * _rms_scale(kg) * kgain_ref[...]).astype(BF16)
    v_ref[...] = kv[:, nk:].astype(BF16)
    q_ref[...] = _dot((normed * qgain_ref[...]).astype(BF16), wq_ref[...]).astype(BF16)


def _dup_kv_columns(w):
    d = w.shape[0]
    w = w.reshape(d, SWA_KV_HEADS, 1, SWA_HEAD_DIM)
    return jnp.broadcast_to(w, (d, SWA_KV_HEADS, 2, SWA_HEAD_DIM)).reshape(d, SWA_KV_HEADS * SWA_KV_LANES)


def _ffn_kvq(h1, fgain, wg, wu, wd, kvgain, kv_w, k_gain, qgain, wq):
    t, d = h1.shape
    rows = min(FFN_ROWS, t)
    nkv = SWA_KV_HEADS * SWA_HEAD_DIM
    kvw2 = jnp.concatenate([_dup_kv_columns(kv_w[:, :nkv]), _dup_kv_columns(kv_w[:, nkv:])],
                           axis=1).astype(BF16)
    kgain2 = jnp.concatenate([k_gain, k_gain]).reshape(1, SWA_KV_LANES)
    nk = SWA_KV_HEADS * SWA_KV_LANES
    row_spec = lambda w: pl.BlockSpec((rows, w), lambda i: (i, 0))
    return pl.pallas_call(
        _ffn_kvq_kernel,
        out_shape=(jax.ShapeDtypeStruct((t, d), F32), jax.ShapeDtypeStruct((t, d), BF16),
                   jax.ShapeDtypeStruct((t, nk), BF16), jax.ShapeDtypeStruct((t, nk), BF16)),
        grid=(t // rows,),
        in_specs=[
            row_spec(d),
            _const_spec((1, d)),
            _const_spec((d, D_FF_DENSE)), _const_spec((d, D_FF_DENSE)), _const_spec((D_FF_DENSE, d)),
            _const_spec((1, d)), _const_spec((d, 2 * nk)), _const_spec((1, SWA_KV_LANES)),
            _const_spec((1, d)), _const_spec((d, d)),
        ],
        out_specs=(row_spec(d), row_spec(d), row_spec(nk), row_spec(nk)),
        compiler_params=pltpu.CompilerParams(
            dimension_semantics=("arbitrary",), vmem_limit_bytes=VMEM_LIMIT_BYTES),
        name="ffn_kvq",
    )(h1, fgain.reshape(1, d), wg, wu, wd, kvgain.reshape(1, d), kvw2, kgain2,
      qgain.reshape(1, d), wq)


ATTN_ROWS = 256
REL_BUCKETS = 32
REL_MAX_DIST = 128
N_EXPERTS = 8
ROUTE_LANES = 128
ROUTE_COLS = 8


def _rel_bucket_table():
    w = SWA_WINDOW
    qi = np.arange(w)[:, None]
    kj = np.arange(2 * w)[None, :]
    dist = qi + w - kj
    in_win = (dist >= 0) & (dist < w)
    n = np.maximum(dist, 0)
    max_exact = REL_BUCKETS // 2
    nf = np.maximum(n, 1).astype(np.float64)
    large = max_exact + (np.log(nf / max_exact) / np.log(REL_MAX_DIST / max_exact)
                         * (REL_BUCKETS - max_exact)).astype(np.int64)
    large = np.minimum(large, REL_BUCKETS - 1)
    bucket = np.where(n < max_exact, n, large)
    return np.where(in_win, bucket, -1).astype(np.int32)


def _attn_router_kernel(relb_ref, sink_ref, q_ref, kc_ref, kp_ref, vc_ref, vp_ref, h2_ref,
                        bucket_ref, qgain_ref, wo_ref, fgain_ref, wr_ref,
                        h3_ref, hn2_ref, route_ref, bias_ref, attn_ref):
    w = SWA_WINDOW
    hd = SWA_HEAD_DIM
    rows = q_ref.shape[1]
    first_tile = pl.program_id(1) == 0

    @pl.when(jnp.logical_and(pl.program_id(0) == 0, first_tile))
    def _():
        bucket = bucket_ref[...]
        for h in range(SWA_Q_HEADS):
            plane = jnp.full((w, 2 * w), -jnp.inf, F32)
            for bkt in range(REL_BUCKETS):
                plane = jnp.where(bucket == bkt, relb_ref[bkt * SWA_Q_HEADS + h], plane)
            bias_ref[h] = plane

    lane = lax.broadcasted_iota(jnp.int32, (2 * w, SWA_KV_LANES), 1)
    left = lane < hd
    lane_q = lax.broadcasted_iota(jnp.int32, (w, SWA_KV_LANES), 1)
    left_q = lane_q < hd
    key_idx = lax.broadcasted_iota(jnp.int32, (1, 2 * w), 1)
    qgain = qgain_ref[...]

    k_all = jnp.concatenate([kp_ref[0], kc_ref[0]], axis=0)
    v_all = jnp.concatenate([vp_ref[0], vc_ref[0]], axis=0)
    zero = jnp.zeros((), BF16)
    no_prev = jnp.where(jnp.logical_and(first_tile, key_idx < w), -jnp.inf, 0.0)
    units = [(blk, g) for blk in range(rows // w) for g in range(SWA_KV_HEADS)]

    score, values = [], []
    for blk, g in units:
        r0 = blk * w
        ks = slice(g * SWA_KV_LANES, (g + 1) * SWA_KV_LANES)
        k2 = k_all[r0:r0 + 2 * w, ks]
        v2 = v_all[r0:r0 + 2 * w, ks]
        k_bd = jnp.concatenate([jnp.where(left, k2, zero), jnp.where(left, zero, k2)], axis=0)
        values.append(jnp.concatenate([jnp.where(left, v2, zero), jnp.where(left, zero, v2)], axis=0))
        pairs = []
        for p in range(2):
            c0 = (4 * g + 2 * p) * hd
            qp = q_ref[0, r0:r0 + w, c0:c0 + 2 * hd].astype(F32)
            sq = qp * qp
            s_l = jnp.sum(jnp.where(left_q, sq, 0.0), axis=-1, keepdims=True)
            s_r = jnp.sum(jnp.where(left_q, 0.0, sq), axis=-1, keepdims=True)
            ms = jnp.where(left_q, s_l, s_r) * (1.0 / hd)
            pairs.append((qp * lax.rsqrt(ms + NORM_EPS) * qgain * (hd ** -0.5)).astype(BF16))
        qs = jnp.concatenate(pairs, axis=0)
        score.append(_dot_nt(qs, k_bd))

    probs, sinks = [], []
    for (blk, g), s in zip(units, score):
        e_rows, sink_rows = [], []
        for p in range(2):
            halves = []
            for side in range(2):
                hq = 4 * g + 2 * p + side
                logit = s[p * w:(p + 1) * w, side * 2 * w:(side + 1) * 2 * w] + bias_ref[hq]
                if blk == 0:
                    logit = logit + no_prev
                sink = sink_ref[hq]
                m = jnp.maximum(jnp.max(logit, axis=-1, keepdims=True), sink)
                halves.append((jnp.exp(logit - m), jnp.exp(sink - m)))
            e_rows.append(jnp.concatenate([halves[0][0], halves[1][0]], axis=1).astype(BF16))
            sink_rows.append(jnp.where(left_q, halves[0][1], halves[1][1]))
        probs.append(jnp.concatenate(e_rows, axis=0))
        sinks.append(jnp.concatenate(sink_rows, axis=0))

    ones_bd = jnp.concatenate([jnp.where(left, 1.0, 0.0), jnp.where(left, 0.0, 1.0)], axis=0).astype(BF16)
    for (blk, g), e2, v_bd, sink_term in zip(units, probs, values, sinks):
        pvd = _dot(e2, jnp.concatenate([v_bd, ones_bd], axis=1))
        pv = pvd[:, :SWA_KV_LANES] / (pvd[:, SWA_KV_LANES:] + sink_term)
        r0 = blk * w
        for p in range(2):
            c0 = (4 * g + 2 * p) * hd
            attn_ref[r0:r0 + w, c0:c0 + 2 * hd] = pv[p * w:(p + 1) * w].astype(BF16)

    h3 = h2_ref[0] + _dot(attn_ref[...], wo_ref[...])
    h3_ref[0] = h3
    hn2 = h3 * _rms_scale(h3) * fgain_ref[...]
    _store_token_rows(hn2_ref, hn2)

    hi = hn2.astype(BF16)
    lo = (hn2 - hi.astype(F32)).astype(BF16)
    logits = _dot(hi, wr_ref[0]) + _dot(lo, wr_ref[0]) + _dot(hi, wr_ref[1])
    col = lax.broadcasted_iota(jnp.int32, logits.shape, 1)
    logits = jnp.where(col < N_EXPERTS, logits, -jnp.inf)
    m0 = jnp.max(logits, axis=-1, keepdims=True)
    i0 = jnp.min(jnp.where(logits == m0, col, ROUTE_LANES), axis=-1, keepdims=True)
    rest = jnp.where(col == i0, -jnp.inf, logits)
    m1 = jnp.max(rest, axis=-1, keepdims=True)
    i1 = jnp.min(jnp.where(rest == m1, col, ROUTE_LANES), axis=-1, keepdims=True)
    e1 = jnp.exp(m1 - m0)
    g0 = 1.0 / (1.0 + e1)
    g1 = e1 / (1.0 + e1)
    rc = lax.broadcasted_iota(jnp.int32, (rows, ROUTE_COLS), 1)
    rec = jnp.where(rc == 0, i0.astype(F32),
                    jnp.where(rc == 1, i1.astype(F32),
                              jnp.where(rc == 2, g0, jnp.where(rc == 3, g1, 0.0))))
    route_ref[0] = rec


def _attn_router(q, k2, v2, h2, rel_bias, sink, q_gain, w_o, fgain, w_router):
    bsz, seq, d = h2.shape
    rows = min(ATTN_ROWS, seq)
    w = SWA_WINDOW
    nk = SWA_KV_HEADS * SWA_KV_LANES
    per = rows // w
    bucket = jnp.asarray(_rel_bucket_table())
    qgain2 = jnp.concatenate([q_gain, q_gain]).reshape(1, SWA_KV_LANES)
    wr = jnp.zeros((d, ROUTE_LANES), F32).at[:, :N_EXPERTS].set(w_router)
    wr_hi = wr.astype(BF16)
    wr_lo = (wr - wr_hi.astype(F32)).astype(BF16)
    wr2 = jnp.stack([wr_hi, wr_lo])
    tile = lambda width: pl.BlockSpec((1, rows, width), lambda b, j, *_: (b, j, 0))
    prev = pl.BlockSpec((1, w, nk), lambda b, j, *_: (b, jnp.maximum(j * per - 1, 0), 0))
    const = lambda shape: pl.BlockSpec(shape, lambda b, j, *_: (0,) * len(shape),
                                       pipeline_mode=pl.Buffered(1))
    return pl.pallas_call(
        _attn_router_kernel,
        out_shape=(jax.ShapeDtypeStruct((bsz, seq, d), F32),
                   jax.ShapeDtypeStruct((bsz * seq * ROW_CHUNKS, 128), F32),
                   jax.ShapeDtypeStruct((bsz, seq, ROUTE_COLS), F32)),
        grid_spec=pltpu.PrefetchScalarGridSpec(
            num_scalar_prefetch=2,
            grid=(bsz, seq // rows),
            in_specs=[tile(d), tile(nk), prev, tile(nk), prev, tile(d),
                      const((w, 2 * w)), const((1, SWA_KV_LANES)), const((d, d)), const((1, d)),
                      const((2, d, ROUTE_LANES))],
            out_specs=(tile(d),
                       pl.BlockSpec((rows * ROW_CHUNKS, 128), lambda b, j, *_: (b * (seq // rows) + j, 0)),
                       tile(ROUTE_COLS)),
            scratch_shapes=[pltpu.VMEM((SWA_Q_HEADS, w, 2 * w), F32), pltpu.VMEM((rows, d), BF16)],
        ),
        compiler_params=pltpu.CompilerParams(
            dimension_semantics=("arbitrary", "arbitrary"), vmem_limit_bytes=VMEM_LIMIT_BYTES),
        name="attn_router",
    )(rel_bias.reshape(-1), sink, q, k2, k2, v2, v2, h2, bucket, qgain2, w_o, fgain.reshape(1, d), wr2)


TOP_K = 2
D_FF_EXPERT = 3584
MOE_TILE = 512
MOE_FF_SUB = 512
RANK_BLOCK = 1024
DISPATCH_BLOCK = 1024
COMBINE_BLOCK = 512
ZERO_ROWS = 256
RANK_SUBLANES = 16
DMA_PRIORITIES = 2


def _rank_kernel(e_ref, tri_ref, rank_ref, cnt_ref, carry_ref):
    bt = e_ref.shape[1]

    @pl.when(pl.program_id(0) == 0)
    def _():
        carry_ref[...] = jnp.zeros_like(carry_ref)

    ex = lax.broadcasted_iota(jnp.int32, (RANK_SUBLANES, bt), 0)
    oh0 = ex == e_ref[0:1, :]
    oh1 = ex == e_ref[1:2, :]
    both = oh0.astype(F32) + oh1.astype(F32)
    pos = _dot(both.astype(BF16), tri_ref[...]) + carry_ref[:, 0:1]
    rank_ref[0:1, :] = jnp.sum(jnp.where(oh0, pos, 0.0), axis=0, keepdims=True).astype(jnp.int32)
    rank_ref[1:2, :] = jnp.sum(jnp.where(oh1, pos, 0.0), axis=0, keepdims=True).astype(jnp.int32)
    carry_ref[...] = carry_ref[...] + jnp.sum(both, axis=1, keepdims=True)
    cnt_ref[...] = carry_ref[...]


def _route_ranks(experts):
    _, t = experts.shape
    bt = min(RANK_BLOCK, t)
    tri = jnp.asarray(np.triu(np.ones((bt, bt), np.float32), k=1), dtype=BF16)
    rank, cnt = pl.pallas_call(
        _rank_kernel,
        out_shape=(jax.ShapeDtypeStruct((TOP_K, t), jnp.int32),
                   jax.ShapeDtypeStruct((RANK_SUBLANES, 128), F32)),
        grid=(t // bt,),
        in_specs=[pl.BlockSpec((TOP_K, bt), lambda i: (0, i)), _const_spec((bt, bt))],
        out_specs=(pl.BlockSpec((TOP_K, bt), lambda i: (0, i)),
                   pl.BlockSpec((RANK_SUBLANES, 128), lambda i: (0, 0))),
        scratch_shapes=[pltpu.VMEM((RANK_SUBLANES, 128), F32)],
        compiler_params=pltpu.CompilerParams(dimension_semantics=("arbitrary",)),
        name="route_rank",
    )(experts, tri)
    return rank, cnt[:N_EXPERTS, 0].astype(jnp.int32)


def _row_copy(src, src_token, dst, dst_token, sem):
    return pltpu.make_async_copy(_token_row_slice(src, src_token), _token_row_slice(dst, dst_token), sem)


def _dispatch_kernel(start_ref, zrow_ref, nz_ref, hn_ref, e_ref, rank_ref, xs_ref, dest_ref,
                     dvm_ref, dsm_ref, zero_ref, sem_idx, sem_rows, sem_zero):
    bt = e_ref.shape[1]
    zero_copies = MOE_TILE // ZERO_ROWS

    def zero_copy(z, c):
        row = pl.multiple_of((zrow_ref[z] + c * ZERO_ROWS) * ROW_CHUNKS, ROW_CHUNKS)
        return pltpu.make_async_copy(zero_ref, xs_ref.at[pl.ds(row, ZERO_ROWS * ROW_CHUNKS)], sem_zero)

    @pl.when(pl.program_id(0) == 0)
    def _():
        zero_ref[...] = jnp.zeros_like(zero_ref)

        def start(z, carry):
            for c in range(zero_copies):
                zero_copy(z, c).start()
            return carry

        def wait(z, carry):
            for c in range(zero_copies):
                zero_copy(z, c).wait()
            return carry

        lax.fori_loop(0, nz_ref[0], start, 0)
        lax.fori_loop(0, nz_ref[0], wait, 0)

    e = e_ref[...]
    base = jnp.zeros_like(e)
    for x in range(N_EXPERTS):
        base = jnp.where(e == x, start_ref[x], base)
    dest = base + rank_ref[...]
    dest_ref[...] = dest
    dvm_ref[...] = dest
    idx_copy = pltpu.make_async_copy(dvm_ref, dsm_ref, sem_idx)
    idx_copy.start()
    idx_copy.wait()

    def issue(t, carry):
        for c in range(TOP_K):
            _row_copy(hn_ref, t, xs_ref, dsm_ref[c, t], sem_rows).start(priority=c % DMA_PRIORITIES)
        return carry

    lax.fori_loop(0, bt, issue, 0, unroll=8)
    for c in range(TOP_K):
        pltpu.make_async_copy(hn_ref, xs_ref.at[pl.ds(0, bt * ROW_CHUNKS)], sem_rows).wait()


def _dispatch(hn, experts, rank, start, zrows, nz, m_pad):
    t = hn.shape[0] // ROW_CHUNKS
    bt = min(DISPATCH_BLOCK, t)
    blk = lambda i, *_: (0, i)
    return pl.pallas_call(
        _dispatch_kernel,
        out_shape=(jax.ShapeDtypeStruct((m_pad * ROW_CHUNKS, 128), F32),
                   jax.ShapeDtypeStruct((TOP_K, t), jnp.int32)),
        grid_spec=pltpu.PrefetchScalarGridSpec(
            num_scalar_prefetch=3,
            grid=(t // bt,),
            in_specs=[pl.BlockSpec((bt * ROW_CHUNKS, 128), lambda i, *_: (i, 0)),
                      pl.BlockSpec((TOP_K, bt), blk), pl.BlockSpec((TOP_K, bt), blk)],
            out_specs=(pl.BlockSpec(memory_space=pl.ANY), pl.BlockSpec((TOP_K, bt), blk)),
            scratch_shapes=[pltpu.VMEM((TOP_K, bt), jnp.int32), pltpu.SMEM((TOP_K, bt), jnp.int32),
                            pltpu.VMEM((ZERO_ROWS * ROW_CHUNKS, 128), F32),
                            pltpu.SemaphoreType.DMA, pltpu.SemaphoreType.DMA, pltpu.SemaphoreType.DMA],
        ),
        compiler_params=pltpu.CompilerParams(dimension_semantics=("arbitrary",)),
        name="moe_dispatch",
    )(start, zrows, nz, hn, experts, rank)


def _expert_kernel(te_ref, tv_ref, x_ref, wg_ref, wu_ref, wd_ref, out_ref):
    valid = tv_ref[pl.program_id(0)] == 1

    @pl.when(valid)
    def _():
        xb = _load_token_rows(x_ref, MOE_TILE).astype(BF16)
        acc = jnp.zeros((MOE_TILE, D_MODEL), F32)
        for j in range(D_FF_EXPERT // MOE_FF_SUB):
            sl = slice(j * MOE_FF_SUB, (j + 1) * MOE_FF_SUB)
            gate = _dot(xb, wg_ref[0, :, sl])
            up = _dot(xb, wu_ref[0, :, sl])
            hid = (gate * _sigmoid(gate) * up).astype(BF16)
            acc = acc + _dot(hid, wd_ref[0, sl, :])
        _store_token_rows(out_ref, acc)

    @pl.when(jnp.logical_not(valid))
    def _():
        out_ref[...] = jnp.zeros_like(out_ref)


def _experts(xs, tile_expert, tile_valid, wg, wu, wd):
    d = D_MODEL
    row_block = pl.BlockSpec((MOE_TILE * ROW_CHUNKS, 128), lambda i, te, tv: (i, 0))
    weights = lambda shape: pl.BlockSpec((1,) + shape, lambda i, te, tv: (te[i], 0, 0),
                                         pipeline_mode=pl.Buffered(1))
    return pl.pallas_call(
        _expert_kernel,
        out_shape=jax.ShapeDtypeStruct(xs.shape, F32),
        grid_spec=pltpu.PrefetchScalarGridSpec(
            num_scalar_prefetch=2,
            grid=(xs.shape[0] // (MOE_TILE * ROW_CHUNKS),),
            in_specs=[row_block, weights((d, D_FF_EXPERT)), weights((d, D_FF_EXPERT)),
                      weights((D_FF_EXPERT, d))],
            out_specs=row_block,
        ),
        compiler_params=pltpu.CompilerParams(
            dimension_semantics=("arbitrary",), vmem_limit_bytes=VMEM_LIMIT_BYTES),
        name="moe_experts",
    )(tile_expert, tile_valid, xs, wg, wu, wd)


def _combine_kernel(h_ref, route_ref, dest_ref, yb_ref, out_ref, dsm_ref, ybuf_ref, sem_idx, sem_rows):
    bt = h_ref.shape[0]
    idx_copy = pltpu.make_async_copy(dest_ref, dsm_ref, sem_idx)
    idx_copy.start()
    idx_copy.wait()

    def issue(t, carry):
        for c in range(TOP_K):
            _row_copy(yb_ref, dsm_ref[c, t], ybuf_ref.at[c], t, sem_rows).start(
                priority=c % DMA_PRIORITIES)
        return carry

    lax.fori_loop(0, bt, issue, 0, unroll=8)
    for c in range(TOP_K):
        pltpu.make_async_copy(yb_ref.at[pl.ds(0, bt * ROW_CHUNKS)], ybuf_ref.at[c], sem_rows).wait()
    route = route_ref[...]
    y0 = _load_token_rows(ybuf_ref, bt, (0,))
    y1 = _load_token_rows(ybuf_ref, bt, (1,))
    out_ref[...] = h_ref[...] + route[:, 2:3] * y0 + route[:, 3:4] * y1


def _combine(h3, route, dest, yb):
    t, d = h3.shape
    bt = min(COMBINE_BLOCK, t)
    return pl.pallas_call(
        _combine_kernel,
        out_shape=jax.ShapeDtypeStruct((t, d), F32),
        grid=(t // bt,),
        in_specs=[pl.BlockSpec((bt, d), lambda i: (i, 0)),
                  pl.BlockSpec((bt, ROUTE_COLS), lambda i: (i, 0)),
                  pl.BlockSpec((TOP_K, bt), lambda i: (0, i)),
                  pl.BlockSpec(memory_space=pl.ANY)],
        out_specs=pl.BlockSpec((bt, d), lambda i: (i, 0)),
        scratch_shapes=[pltpu.SMEM((TOP_K, bt), jnp.int32),
                        pltpu.VMEM((TOP_K, bt * ROW_CHUNKS, 128), F32),
                        pltpu.SemaphoreType.DMA, pltpu.SemaphoreType.DMA],
        compiler_params=pltpu.CompilerParams(
            dimension_semantics=("arbitrary",), vmem_limit_bytes=VMEM_LIMIT_BYTES),
        name="moe_combine",
    )(h3, route, dest, yb)


def _moe_layer(h3, hn2, route, wg, wu, wd):
    t, d = h3.shape
    n_tiles = (t * TOP_K) // MOE_TILE + N_EXPERTS
    m_pad = n_tiles * MOE_TILE
    experts = route[:, :TOP_K].astype(jnp.int32).T
    rank, counts = _route_ranks(experts)
    tiles_per = (counts + MOE_TILE - 1) // MOE_TILE
    tile_end = jnp.cumsum(tiles_per)
    start = ((tile_end - tiles_per) * MOE_TILE).astype(jnp.int32)
    tile_ids = jnp.arange(n_tiles, dtype=jnp.int32)
    tile_valid = (tile_ids < tile_end[-1]).astype(jnp.int32)
    last_valid = jnp.maximum(tile_end[-1] - 1, 0)
    tile_expert = jnp.minimum(
        jnp.searchsorted(tile_end, jnp.minimum(tile_ids, last_valid), side="right"),
        N_EXPERTS - 1).astype(jnp.int32)
    has_pad = (counts % MOE_TILE) != 0
    pad_tile = jnp.where(has_pad, tile_end - 1, n_tiles)
    zmask = jnp.zeros((n_tiles + 1,), jnp.int32).at[pad_tile].set(1)[:n_tiles]
    zmask = jnp.maximum(zmask, 1 - tile_valid)
    nz = jnp.sum(zmask).astype(jnp.int32).reshape(1)
    zorder = jnp.argsort(1 - zmask, stable=True).astype(jnp.int32)
    zrows = (zorder[:2 * N_EXPERTS] * MOE_TILE).astype(jnp.int32)
    xs, dest = _dispatch(hn2, experts, rank, start, zrows, nz, m_pad)
    yb = _experts(xs, tile_expert, tile_valid, wg, wu, wd)
    return _combine(h3, route, dest, yb)


def _stages(x, hgrn_w_in, hgrn_lb, hgrn_gnorm, hgrn_w_out, swa_w_q, swa_q_gain, swa_sink, swa_w_o, kv_norm, kv_w, k_gain, rel_bias, attn_norm, ffn_norm, ffn_w_gate, ffn_w_up, ffn_w_down, moe_router, moe_w_gate, moe_w_up, moe_w_down):
    bsz, seq, d = x.shape
    t = bsz * seq
    bf = lambda w: w.astype(BF16)
    h1 = _hgrn_layer(x, attn_norm[0], bf(hgrn_w_in[0]), hgrn_lb, hgrn_gnorm[0], bf(hgrn_w_out[0]))
    h2, q, k2, v2 = _ffn_kvq(h1.reshape(t, d), ffn_norm[0], bf(ffn_w_gate[0]), bf(ffn_w_up[0]),
                             bf(ffn_w_down[0]), kv_norm, kv_w, k_gain, attn_norm[1], bf(swa_w_q[0]))
    nk = SWA_KV_HEADS * SWA_KV_LANES
    h3, hn2, route = _attn_router(q.reshape(bsz, seq, d), k2.reshape(bsz, seq, nk),
                                  v2.reshape(bsz, seq, nk), h2.reshape(bsz, seq, d), rel_bias,
                                  swa_sink[0], swa_q_gain[0], bf(swa_w_o[0]), ffn_norm[1],
                                  moe_router[0])
    h4 = _moe_layer(h3.reshape(t, d), hn2, route.reshape(t, ROUTE_COLS),
                    bf(moe_w_gate[0]), bf(moe_w_up[0]), bf(moe_w_down[0]))
    return {"h1": h1, "h2": h2.reshape(bsz, seq, d), "h3": h3, "h4": h4.reshape(bsz, seq, d)}


def kernel(x, hgrn_w_in, hgrn_lb, hgrn_gnorm, hgrn_w_out, swa_w_q, swa_q_gain, swa_sink, swa_w_o, kv_norm, kv_w, k_gain, rel_bias, attn_norm, ffn_norm, ffn_w_gate, ffn_w_up, ffn_w_down, moe_router, moe_w_gate, moe_w_up, moe_w_down):
    return _stages(x, hgrn_w_in, hgrn_lb, hgrn_gnorm, hgrn_w_out, swa_w_q, swa_q_gain, swa_sink, swa_w_o, kv_norm, kv_w, k_gain, rel_bias, attn_norm, ffn_norm, ffn_w_gate, ffn_w_up, ffn_w_down, moe_router, moe_w_gate, moe_w_up, moe_w_down)["h4"]
```

```python
import functools

import numpy as np
import jax
import jax.numpy as jnp
from jax import lax
from jax.experimental import pallas as pl
from jax.experimental.pallas import tpu as pltpu

F32 = jnp.float32
BF16 = jnp.bfloat16

D_MODEL = 1024
NORM_EPS = 1e-6
LOG2_E = 1.4426950408889634

HG_HEADS = 8
HG_DK = 128
HG_DV = 128
HG_CHUNK = 128
HG_LEVELS = 7
HG_STEP_CHUNKS = 2

VMEM_LIMIT_BYTES = 56 * 1024 * 1024


def _dot(a, b):
    return jnp.dot(a, b, preferred_element_type=F32)


def _dot_nt(a, b):
    return lax.dot_general(a, b, (((1,), (1,)), ((), ())), preferred_element_type=F32)


def _dot_tn(a, b):
    return lax.dot_general(a, b, (((0,), (0,)), ((), ())), preferred_element_type=F32)


def _rms_scale(x):
    return lax.rsqrt(jnp.mean(x * x, axis=-1, keepdims=True) + NORM_EPS)


def _sigmoid(x):
    return 1.0 / (1.0 + jnp.exp(-x))


ROW_CHUNKS = D_MODEL // 128


def _load_token_rows(ref, n, lead=()):
    return jnp.concatenate(
        [ref[lead + (pl.ds(c, n, stride=ROW_CHUNKS), slice(None))] for c in range(ROW_CHUNKS)], axis=1)


def _store_token_rows(ref, x):
    n = x.shape[0]
    for c in range(ROW_CHUNKS):
        ref[pl.ds(c, n, stride=ROW_CHUNKS), :] = x[:, c * 128:(c + 1) * 128]


def _token_row_slice(ref, token):
    return ref.at[pl.ds(pl.multiple_of(token * ROW_CHUNKS, ROW_CHUNKS), ROW_CHUNKS)]


def _const_spec(shape):
    nd = len(shape)
    return pl.BlockSpec(shape, lambda *_: (0,) * nd, pipeline_mode=pl.Buffered(1))


def _hgrn_level_log_decay(b, level):
    c = HG_CHUNK
    m = 1 << level
    parts = []
    if 2 * m >= 8:
        for start in range(0, c, 2 * m):
            mid = start + m - 1
            parts.append(b[start:start + 2 * m] - b[mid:mid + 1])
    else:
        first_block = lax.broadcasted_iota(jnp.int32, (8, b.shape[1]), 0) < 4
        for start in range(0, c, 8):
            b_mid = jnp.where(first_block, b[start + 1:start + 2], b[start + 5:start + 6])
            parts.append(b[start:start + 8] - b_mid)
    return jnp.concatenate(parts, axis=0)


def _hgrn_kernel(x_ref, gain_ref, win_ref, lbp_ref, tsum_ref, gnorm_ref, wout_ref,
                 out_ref, st_ref, o_ref):
    c = HG_CHUNK

    @pl.when(pl.program_id(1) == 0)
    def _():
        st_ref[...] = jnp.zeros_like(st_ref)

    chunks = [slice(ci * c, (ci + 1) * c) for ci in range(HG_STEP_CHUNKS)]

    lbp = lbp_ref[...]
    lbe = jnp.exp(lbp - jnp.max(lbp, axis=0, keepdims=True))
    lb = lbe[0:1] / jnp.sum(lbe, axis=0, keepdims=True)

    x = x_ref[0]
    hn = (x * _rms_scale(x) * gain_ref[...]).astype(BF16)
    projs = [_dot(hn[rs], win_ref[...]) for rs in chunks]
    gated = []
    for proj in projs:
        f = lb + (1.0 - lb) * _sigmoid(proj[:, 1 * D_MODEL:2 * D_MODEL])
        g = jnp.log(f)
        g_hi = g.astype(BF16)
        g_lo = (g - g_hi.astype(F32)).astype(BF16)
        gated.append((f, _dot(tsum_ref[...], jnp.concatenate([g_hi, g_lo], axis=0))))

    ti = lax.broadcasted_iota(jnp.int32, (c, c), 0)
    si = lax.broadcasted_iota(jnp.int32, (c, c), 1)
    diag = ti == si
    xor = ti ^ si
    lv_mask = [((xor >> l) == 1) & (((ti >> l) & 1) == 1) for l in range(HG_LEVELS)]
    keep = [jnp.where(m, 1.0, 0.0) for m in [diag] + lv_mask]

    heads = [slice(h * HG_DK, (h + 1) * HG_DK) for h in range(HG_HEADS)]

    row = lax.broadcasted_iota(jnp.int32, (c, HG_DK), 0)
    upper_rows = [((row >> l) & 1) == 1 for l in range(HG_LEVELS)]
    signs = [jnp.where(u, LOG2_E, -LOG2_E) for u in upper_rows]

    staged = []
    for proj, (f, b) in zip(projs, gated):
        per_chunk = []
        for h, hs in enumerate(heads):
            q_h = proj[:, hs]
            f_h = f[:, hs]
            b_h = b[:, hs]
            k_h = 1.0 - f_h
            b_last = b_h[c - 1:c]
            per_level = []
            for l in range(HG_LEVELS):
                if l == 0:
                    w = jnp.where(upper_rows[0], f_h, 1.0)
                else:
                    w = jnp.exp2(_hgrn_level_log_decay(b_h, l) * signs[l])
                per_level.append((jnp.where(upper_rows[l], q_h, k_h) * w).astype(BF16))
            per_chunk.append(dict(
                levels=per_level, q=q_h.astype(BF16), k=k_h.astype(BF16),
                qb=(q_h * jnp.exp(b_h)).astype(BF16),
                khat=(k_h * jnp.exp(b_last - b_h)).astype(BF16),
                decay=jnp.exp(b_last),
                v=proj[:, 2 * D_MODEL + h * HG_DV:2 * D_MODEL + (h + 1) * HG_DV].astype(BF16),
                o_gate=proj[:, 3 * D_MODEL + h * HG_DV:3 * D_MODEL + (h + 1) * HG_DV]))
        staged.append(per_chunk)

    scores = []
    for per_chunk in staged:
        per_head = []
        for s in per_chunk:
            a = _dot_nt(s["q"], s["k"]) * keep[0]
            for m, lv in zip(keep[1:], s["levels"]):
                a = a + _dot_nt(lv, lv) * m
            per_head.append(a.astype(BF16))
        scores.append(per_head)

    outs = []
    for per_chunk, per_head in zip(staged, scores):
        o_heads = []
        for h, s in enumerate(per_chunk):
            st = st_ref[h]
            o_heads.append(_dot(per_head[h], s["v"]) + _dot_nt(s["qb"], st.astype(BF16)))
            st_ref[h] = st * s["decay"] + _dot_tn(s["v"], s["khat"])
        outs.append(o_heads)

    for rs, per_chunk, o_heads in zip(chunks, staged, outs):
        for h, hs in enumerate(heads):
            o_h = o_heads[h] * _rms_scale(o_heads[h]) * gnorm_ref[...]
            og = per_chunk[h]["o_gate"]
            o_ref[rs, hs] = (o_h * (og * _sigmoid(og))).astype(BF16)

    out_ref[0] = x + _dot(o_ref[...], wout_ref[...])


def _hgrn_layer(x, gain, w_in, lb_param, gnorm, w_out):
    bsz, seq, d = x.shape
    c = HG_CHUNK * HG_STEP_CHUNKS
    tril = np.tril(np.ones((HG_CHUNK, HG_CHUNK), np.float32))
    tsum2 = jnp.asarray(np.concatenate([tril, tril], axis=1), dtype=BF16)
    return pl.pallas_call(
        _hgrn_kernel,
        out_shape=jax.ShapeDtypeStruct((bsz, seq, d), F32),
        grid=(bsz, seq // c),
        in_specs=[
            pl.BlockSpec((1, c, d), lambda b, j: (b, j, 0)),
            _const_spec((1, d)),
            _const_spec((d, 4 * d)),
            _const_spec(lb_param.shape),
            _const_spec(tsum2.shape),
            _const_spec((1, HG_DV)),
            _const_spec((d, d)),
        ],
        out_specs=pl.BlockSpec((1, c, d), lambda b, j: (b, j, 0)),
        scratch_shapes=[
            pltpu.VMEM((HG_HEADS, HG_DV, HG_DK), F32),
            pltpu.VMEM((c, d), BF16),
        ],
        compiler_params=pltpu.CompilerParams(
            dimension_semantics=("arbitrary", "arbitrary"),
            vmem_limit_bytes=VMEM_LIMIT_BYTES),
        name="hgrn_layer",
    )(x, gain.reshape(1, d), w_in, lb_param, tsum2, gnorm.reshape(1, HG_DV), w_out)


D_FF_DENSE = 2816
FFN_FF_SPLITS = (0, 768, 1536, 2304, 2816)
FFN_ROWS = 512

SWA_HEAD_DIM = 64
SWA_Q_HEADS = 16
SWA_KV_HEADS = 4
SWA_WINDOW = 128
SWA_KV_LANES = 2 * SWA_HEAD_DIM


def _ffn_kvq_kernel(h_ref, fgain_ref, wg_ref, wu_ref, wd_ref, kvgain_ref, kvw_ref, kgain_ref,
                    qgain_ref, wq_ref, h2_ref, q_ref, k_ref, v_ref):
    h = h_ref[...]
    hn = (h * _rms_scale(h) * fgain_ref[...]).astype(BF16)
    acc = h
    for c in range(len(FFN_FF_SPLITS) - 1):
        sl = slice(FFN_FF_SPLITS[c], FFN_FF_SPLITS[c + 1])
        gate = _dot(hn, wg_ref[:, sl])
        up = _dot(hn, wu_ref[:, sl])
        hid = (gate * _sigmoid(gate) * up).astype(BF16)
        acc = acc + _dot(hid, wd_ref[sl, :])
    h2_ref[...] = acc
    normed = acc * _rms_scale(acc)
    kv = _dot((normed * kvgain_ref[...]).astype(BF16), kvw_ref[...])
    nk = SWA_KV_HEADS * SWA_KV_LANES
    for g in range(SWA_KV_HEADS):
        sl = slice(g * SWA_KV_LANES, (g + 1) * SWA_KV_LANES)
        kg = kv[:, sl]
        k_ref[:, sl] = (kg * _rms_scale(kg) * kgain_ref[...]).astype(BF16)
    v_ref[...] = kv[:, nk:].astype(BF16)
    q_ref[...] = _dot((normed * qgain_ref[...]).astype(BF16), wq_ref[...]).astype(BF16)


def _dup_kv_columns(w):
    d = w.shape[0]
    w = w.reshape(d, SWA_KV_HEADS, 1, SWA_HEAD_DIM)
    return jnp.broadcast_to(w, (d, SWA_KV_HEADS, 2, SWA_HEAD_DIM)).reshape(d, SWA_KV_HEADS * SWA_KV_LANES)


def _ffn_kvq(h1, fgain, wg, wu, wd, kvgain, kv_w, k_gain, qgain, wq):
    t, d = h1.shape
    rows = min(FFN_ROWS, t)
    nkv = SWA_KV_HEADS * SWA_HEAD_DIM
    kvw2 = jnp.concatenate([_dup_kv_columns(kv_w[:, :nkv]), _dup_kv_columns(kv_w[:, nkv:])],
                           axis=1).astype(BF16)
    kgain2 = jnp.concatenate([k_gain, k_gain]).reshape(1, SWA_KV_LANES)
    nk = SWA_KV_HEADS * SWA_KV_LANES
    row_spec = lambda w: pl.BlockSpec((rows, w), lambda i: (i, 0))
    return pl.pallas_call(
        _ffn_kvq_kernel,
        out_shape=(jax.ShapeDtypeStruct((t, d), F32), jax.ShapeDtypeStruct((t, d), BF16),
                   jax.ShapeDtypeStruct((t, nk), BF16), jax.ShapeDtypeStruct((t, nk), BF16)),
        grid=(t // rows,),
        in_specs=[
            row_spec(d),
            _const_spec((1, d)),
            _const_spec((d, D_FF_DENSE)), _const_spec((d, D_FF_DENSE)), _const_spec((D_FF_DENSE, d)),
            _const_spec((1, d)), _const_spec((d, 2 * nk)), _const_spec((1, SWA_KV_LANES)),
            _const_spec((1, d)), _const_spec((d, d)),
        ],
        out_specs=(row_spec(d), row_spec(d), row_spec(nk), row_spec(nk)),
        compiler_params=pltpu.CompilerParams(
            dimension_semantics=("arbitrary",), vmem_limit_bytes=VMEM_LIMIT_BYTES),
        name="ffn_kvq",
    )(h1, fgain.reshape(1, d), wg, wu, wd, kvgain.reshape(1, d), kvw2, kgain2,
      qgain.reshape(1, d), wq)


ATTN_ROWS = 256
REL_BUCKETS = 32
REL_MAX_DIST = 128
N_EXPERTS = 8
ROUTE_LANES = 128
ROUTE_COLS = 8


def _rel_bucket_table():
    w = SWA_WINDOW
    qi = np.arange(w)[:, None]
    kj = np.arange(2 * w)[None, :]
    dist = qi + w - kj
    in_win = (dist >= 0) & (dist < w)
    n = np.maximum(dist, 0)
    max_exact = REL_BUCKETS // 2
    nf = np.maximum(n, 1).astype(np.float64)
    large = max_exact + (np.log(nf / max_exact) / np.log(REL_MAX_DIST / max_exact)
                         * (REL_BUCKETS - max_exact)).astype(np.int64)
    large = np.minimum(large, REL_BUCKETS - 1)
    bucket = np.where(n < max_exact, n, large)
    return np.where(in_win, bucket, -1).astype(np.int32)


def _attn_router_kernel(relb_ref, sink_ref, q_ref, kc_ref, kp_ref, vc_ref, vp_ref, h2_ref,
                        bucket_ref, qgain_ref, wo_ref, fgain_ref, wr_ref,
                        h3_ref, hn2_ref, route_ref, bias_ref, attn_ref):
    w = SWA_WINDOW
    hd = SWA_HEAD_DIM
    rows = q_ref.shape[1]
    first_tile = pl.program_id(1) == 0

    @pl.when(jnp.logical_and(pl.program_id(0) == 0, first_tile))
    def _():
        bucket = bucket_ref[...]
        for h in range(SWA_Q_HEADS):
            plane = jnp.full((w, 2 * w), -jnp.inf, F32)
            for bkt in range(REL_BUCKETS):
                plane = jnp.where(bucket == bkt, relb_ref[bkt * SWA_Q_HEADS + h], plane)
            bias_ref[h] = plane

    lane = lax.broadcasted_iota(jnp.int32, (2 * w, SWA_KV_LANES), 1)
    left = lane < hd
    lane_q = lax.broadcasted_iota(jnp.int32, (w, SWA_KV_LANES), 1)
    left_q = lane_q < hd
    key_idx = lax.broadcasted_iota(jnp.int32, (1, 2 * w), 1)
    qgain = qgain_ref[...]

    k_all = jnp.concatenate([kp_ref[0], kc_ref[0]], axis=0)
    v_all = jnp.concatenate([vp_ref[0], vc_ref[0]], axis=0)
    zero = jnp.zeros((), BF16)
    no_prev = jnp.where(jnp.logical_and(first_tile, key_idx < w), -jnp.inf, 0.0)
    units = [(blk, g) for blk in range(rows // w) for g in range(SWA_KV_HEADS)]

    score, values = [], []
    for blk, g in units:
        r0 = blk * w
        ks = slice(g * SWA_KV_LANES, (g + 1) * SWA_KV_LANES)
        k2 = k_all[r0:r0 + 2 * w, ks]
        v2 = v_all[r0:r0 + 2 * w, ks]
        k_bd = jnp.concatenate([jnp.where(left, k2, zero), jnp.where(left, zero, k2)], axis=0)
        values.append(jnp.concatenate([jnp.where(left, v2, zero), jnp.where(left, zero, v2)], axis=0))
        pairs = []
        for p in range(2):
            c0 = (4 * g + 2 * p) * hd
            qp = q_ref[0, r0:r0 + w, c0:c0 + 2 * hd].astype(F32)
            sq = qp * qp
            s_l = jnp.sum(jnp.where(left_q, sq, 0.0), axis=-1, keepdims=True)
            s_r = jnp.sum(jnp.where(left_q, 0.0, sq), axis=-1, keepdims=True)
            ms = jnp.where(left_q, s_l, s_r) * (1.0 / hd)
            pairs.append((qp * lax.rsqrt(ms + NORM_EPS) * qgain * (hd ** -0.5)).astype(BF16))
        qs = jnp.concatenate(pairs, axis=0)
        score.append(_dot_nt(qs, k_bd))

    probs, sinks = [], []
    for (blk, g), s in zip(units, score):
        e_rows, sink_rows = [], []
        for p in range(2):
            halves = []
            for side in range(2):
                hq = 4 * g + 2 * p + side
                logit = s[p * w:(p + 1) * w, side * 2 * w:(side + 1) * 2 * w] + bias_ref[hq]
                if blk == 0:
                    logit = logit + no_prev
                sink = sink_ref[hq]
                m = jnp.maximum(jnp.max(logit, axis=-1, keepdims=True), sink)
                halves.append((jnp.exp(logit - m), jnp.exp(sink - m)))
            e_rows.append(jnp.concatenate([halves[0][0], halves[1][0]], axis=1).astype(BF16))
            sink_rows.append(jnp.where(left_q, halves[0][1], halves[1][1]))
        probs.append(jnp.concatenate(e_rows, axis=0))
        sinks.append(jnp.concatenate(sink_rows, axis=0))

    ones_bd = jnp.concatenate([jnp.where(left, 1.0, 0.0), jnp.where(left, 0.0, 1.0)], axis=0).astype(BF16)
    for (blk, g), e2, v_bd, sink_term in zip(units, probs, values, sinks):
        pvd = _dot(e2, jnp.concatenate([v_bd, ones_bd], axis=1))
        pv = pvd[:, :SWA_KV_LANES] / (pvd[:, SWA_KV_LANES:] + sink_term)
        r0 = blk * w
        for p in range(2):
            c0 = (4 * g + 2 * p) * hd
            attn_ref[r0:r0 + w, c0:c0 + 2 * hd] = pv[p * w:(p + 1) * w].astype(BF16)

    h3 = h2_ref[0] + _dot(attn_ref[...], wo_ref[...])
    h3_ref[0] = h3
    hn2 = h3 * _rms_scale(h3) * fgain_ref[...]
    _store_token_rows(hn2_ref, hn2)

    hi = hn2.astype(BF16)
    lo = (hn2 - hi.astype(F32)).astype(BF16)
    logits = _dot(hi, wr_ref[0]) + _dot(lo, wr_ref[0]) + _dot(hi, wr_ref[1])
    col = lax.broadcasted_iota(jnp.int32, logits.shape, 1)
    logits = jnp.where(col < N_EXPERTS, logits, -jnp.inf)
    m0 = jnp.max(logits, axis=-1, keepdims=True)
    i0 = jnp.min(jnp.where(logits == m0, col, ROUTE_LANES), axis=-1, keepdims=True)
    rest = jnp.where(col == i0, -jnp.inf, logits)
    m1 = jnp.max(rest, axis=-1, keepdims=True)
    i1 = jnp.min(jnp.where(rest == m1, col, ROUTE_LANES), axis=-1, keepdims=True)
    e1 = jnp.exp(m1 - m0)
    g0 = 1.0 / (1.0 + e1)
    g1 = e1 / (1.0 + e1)
    rc = lax.broadcasted_iota(jnp.int32, (rows, ROUTE_COLS), 1)
    rec = jnp.where(rc == 0, i0.astype(F32),
                    jnp.where(rc == 1, i1.astype(F32),
                              jnp.where(rc == 2, g0, jnp.where(rc == 3, g1, 0.0))))
    route_ref[0] = rec


def _attn_router(q, k2, v2, h2, rel_bias, sink, q_gain, w_o, fgain, w_router):
    bsz, seq, d = h2.shape
    rows = min(ATTN_ROWS, seq)
    w = SWA_WINDOW
    nk = SWA_KV_HEADS * SWA_KV_LANES
    per = rows // w
    bucket = jnp.asarray(_rel_bucket_table())
    qgain2 = jnp.concatenate([q_gain, q_gain]).reshape(1, SWA_KV_LANES)
    wr = jnp.zeros((d, ROUTE_LANES), F32).at[:, :N_EXPERTS].set(w_router)
    wr_hi = wr.astype(BF16)
    wr_lo = (wr - wr_hi.astype(F32)).astype(BF16)
    wr2 = jnp.stack([wr_hi, wr_lo])
    tile = lambda width: pl.BlockSpec((1, rows, width), lambda b, j, *_: (b, j, 0))
    prev = pl.BlockSpec((1, w, nk), lambda b, j, *_: (b, jnp.maximum(j * per - 1, 0), 0))
    const = lambda shape: pl.BlockSpec(shape, lambda b, j, *_: (0,) * len(shape),
                                       pipeline_mode=pl.Buffered(1))
    return pl.pallas_call(
        _attn_router_kernel,
        out_shape=(jax.ShapeDtypeStruct((bsz, seq, d), F32),
                   jax.ShapeDtypeStruct((bsz * seq * ROW_CHUNKS, 128), F32),
                   jax.ShapeDtypeStruct((bsz, seq, ROUTE_COLS), F32)),
        grid_spec=pltpu.PrefetchScalarGridSpec(
            num_scalar_prefetch=2,
            grid=(bsz, seq // rows),
            in_specs=[tile(d), tile(nk), prev, tile(nk), prev, tile(d),
                      const((w, 2 * w)), const((1, SWA_KV_LANES)), const((d, d)), const((1, d)),
                      const((2, d, ROUTE_LANES))],
            out_specs=(tile(d),
                       pl.BlockSpec((rows * ROW_CHUNKS, 128), lambda b, j, *_: (b * (seq // rows) + j, 0)),
                       tile(ROUTE_COLS)),
            scratch_shapes=[pltpu.VMEM((SWA_Q_HEADS, w, 2 * w), F32), pltpu.VMEM((rows, d), BF16)],
        ),
        compiler_params=pltpu.CompilerParams(
            dimension_semantics=("arbitrary", "arbitrary"), vmem_limit_bytes=VMEM_LIMIT_BYTES),
        name="attn_router",
    )(rel_bias.reshape(-1), sink, q, k2, k2, v2, v2, h2, bucket, qgain2, w_o, fgain.reshape(1, d), wr2)


TOP_K = 2
D_FF_EXPERT = 3584
MOE_TILE = 512
MOE_FF_SUB = 512
RANK_BLOCK = 1024
DISPATCH_BLOCK = 1024
COMBINE_BLOCK = 512
ZERO_ROWS = 256
RANK_SUBLANES = 16
DMA_PRIORITIES = 2


def _rank_kernel(e_ref, tri_ref, rank_ref, cnt_ref, carry_ref):
    bt = e_ref.shape[1]

    @pl.when(pl.program_id(0) == 0)
    def _():
        carry_ref[...] = jnp.zeros_like(carry_ref)

    ex = lax.broadcasted_iota(jnp.int32, (RANK_SUBLANES, bt), 0)
    oh0 = ex == e_ref[0:1, :]
    oh1 = ex == e_ref[1:2, :]
    both = oh0.astype(F32) + oh1.astype(F32)
    pos = _dot(both.astype(BF16), tri_ref[...]) + carry_ref[:, 0:1]
    rank_ref[0:1, :] = jnp.sum(jnp.where(oh0, pos, 0.0), axis=0, keepdims=True).astype(jnp.int32)
    rank_ref[1:2, :] = jnp.sum(jnp.where(oh1, pos, 0.0), axis=0, keepdims=True).astype(jnp.int32)
    carry_ref[...] = carry_ref[...] + jnp.sum(both, axis=1, keepdims=True)
    cnt_ref[...] = carry_ref[...]


def _route_ranks(experts):
    _, t = experts.shape
    bt = min(RANK_BLOCK, t)
    tri = jnp.asarray(np.triu(np.ones((bt, bt), np.float32), k=1), dtype=BF16)
    rank, cnt = pl.pallas_call(
        _rank_kernel,
        out_shape=(jax.ShapeDtypeStruct((TOP_K, t), jnp.int32),
                   jax.ShapeDtypeStruct((RANK_SUBLANES, 128), F32)),
        grid=(t // bt,),
        in_specs=[pl.BlockSpec((TOP_K, bt), lambda i: (0, i)), _const_spec((bt, bt))],
        out_specs=(pl.BlockSpec((TOP_K, bt), lambda i: (0, i)),
                   pl.BlockSpec((RANK_SUBLANES, 128), lambda i: (0, 0))),
        scratch_shapes=[pltpu.VMEM((RANK_SUBLANES, 128), F32)],
        compiler_params=pltpu.CompilerParams(dimension_semantics=("arbitrary",)),
        name="route_rank",
    )(experts, tri)
    return rank, cnt[:N_EXPERTS, 0].astype(jnp.int32)


def _row_copy(src, src_token, dst, dst_token, sem):
    return pltpu.make_async_copy(_token_row_slice(src, src_token), _token_row_slice(dst, dst_token), sem)


def _dispatch_kernel(start_ref, zrow_ref, nz_ref, hn_ref, e_ref, rank_ref, xs_ref, dest_ref,
                     dvm_ref, dsm_ref, zero_ref, sem_idx, sem_rows, sem_zero):
    bt = e_ref.shape[1]
    zero_copies = MOE_TILE // ZERO_ROWS

    def zero_copy(z, c):
        row = pl.multiple_of((zrow_ref[z] + c * ZERO_ROWS) * ROW_CHUNKS, ROW_CHUNKS)
        return pltpu.make_async_copy(zero_ref, xs_ref.at[pl.ds(row, ZERO_ROWS * ROW_CHUNKS)], sem_zero)

    @pl.when(pl.program_id(0) == 0)
    def _():
        zero_ref[...] = jnp.zeros_like(zero_ref)

        def start(z, carry):
            for c in range(zero_copies):
                zero_copy(z, c).start()
            return carry

        def wait(z, carry):
            for c in range(zero_copies):
                zero_copy(z, c).wait()
            return carry

        lax.fori_loop(0, nz_ref[0], start, 0)
        lax.fori_loop(0, nz_ref[0], wait, 0)

    e = e_ref[...]
    base = jnp.zeros_like(e)
    for x in range(N_EXPERTS):
        base = jnp.where(e == x, start_ref[x], base)
    dest = base + rank_ref[...]
    dest_ref[...] = dest
    dvm_ref[...] = dest
    idx_copy = pltpu.make_async_copy(dvm_ref, dsm_ref, sem_idx)
    idx_copy.start()
    idx_copy.wait()

    def issue(t, carry):
        for c in range(TOP_K):
            _row_copy(hn_ref, t, xs_ref, dsm_ref[c, t], sem_rows).start(priority=c % DMA_PRIORITIES)
        return carry

    lax.fori_loop(0, bt, issue, 0, unroll=8)
    for c in range(TOP_K):
        pltpu.make_async_copy(hn_ref, xs_ref.at[pl.ds(0, bt * ROW_CHUNKS)], sem_rows).wait()


def _dispatch(hn, experts, rank, start, zrows, nz, m_pad):
    t = hn.shape[0] // ROW_CHUNKS
    bt = min(DISPATCH_BLOCK, t)
    blk = lambda i, *_: (0, i)
    return pl.pallas_call(
        _dispatch_kernel,
        out_shape=(jax.ShapeDtypeStruct((m_pad * ROW_CHUNKS, 128), F32),
                   jax.ShapeDtypeStruct((TOP_K, t), jnp.int32)),
        grid_spec=pltpu.PrefetchScalarGridSpec(
            num_scalar_prefetch=3,
            grid=(t // bt,),
            in_specs=[pl.BlockSpec((bt * ROW_CHUNKS, 128), lambda i, *_: (i, 0)),
                      pl.BlockSpec((TOP_K, bt), blk), pl.BlockSpec((TOP_K, bt), blk)],
            out_specs=(pl.BlockSpec(memory_space=pl.ANY), pl.BlockSpec((TOP_K, bt), blk)),
            scratch_shapes=[pltpu.VMEM((TOP_K, bt), jnp.int32), pltpu.SMEM((TOP_K, bt), jnp.int32),
                            pltpu.VMEM((ZERO_ROWS * ROW_CHUNKS, 128), F32),
                            pltpu.SemaphoreType.DMA, pltpu.SemaphoreType.DMA, pltpu.SemaphoreType.DMA],
        ),
        compiler_params=pltpu.CompilerParams(dimension_semantics=("arbitrary",)),
        name="moe_dispatch",
    )(start, zrows, nz, hn, experts, rank)


def _expert_kernel(te_ref, tv_ref, x_ref, wg_ref, wu_ref, wd_ref, out_ref):
    valid = tv_ref[pl.program_id(0)] == 1

    @pl.when(valid)
    def _():
        xb = _load_token_rows(x_ref, MOE_TILE).astype(BF16)
        acc = jnp.zeros((MOE_TILE, D_MODEL), F32)
        for j in range(D_FF_EXPERT // MOE_FF_SUB):
            sl = slice(j * MOE_FF_SUB, (j + 1) * MOE_FF_SUB)
            gate = _dot(xb, wg_ref[0, :, sl])
            up = _dot(xb, wu_ref[0, :, sl])
            hid = (gate * _sigmoid(gate) * up).astype(BF16)
            acc = acc + _dot(hid, wd_ref[0, sl, :])
        _store_token_rows(out_ref, acc)

    @pl.when(jnp.logical_not(valid))
    def _():
        out_ref[...] = jnp.zeros_like(out_ref)


def _experts(xs, tile_expert, tile_valid, wg, wu, wd):
    d = D_MODEL
    row_block = pl.BlockSpec((MOE_TILE * ROW_CHUNKS, 128), lambda i, te, tv: (i, 0))
    weights = lambda shape: pl.BlockSpec((1,) + shape, lambda i, te, tv: (te[i], 0, 0),
                                         pipeline_mode=pl.Buffered(1))
    return pl.pallas_call(
        _expert_kernel,
        out_shape=jax.ShapeDtypeStruct(xs.shape, F32),
        grid_spec=pltpu.PrefetchScalarGridSpec(
            num_scalar_prefetch=2,
            grid=(xs.shape[0] // (MOE_TILE * ROW_CHUNKS),),
            in_specs=[row_block, weights((d, D_FF_EXPERT)), weights((d, D_FF_EXPERT)),
                      weights((D_FF_EXPERT, d))],
            out_specs=row_block,
        ),
        compiler_params=pltpu.CompilerParams(
            dimension_semantics=("arbitrary",), vmem_limit_bytes=VMEM_LIMIT_BYTES),
        name="moe_experts",
    )(tile_expert, tile_valid, xs, wg, wu, wd)


def _combine_kernel(h_ref, route_ref, dest_ref, dest_next_ref, yb_ref, out_ref,
                    dsm_ref, ybuf_ref, sem_idx, sem_rows):
    bt = h_ref.shape[0]
    i = pl.program_id(0)
    slot = i % 2

    def start_gathers(step_dest_ref, s):
        idx_copy = pltpu.make_async_copy(step_dest_ref, dsm_ref.at[s], sem_idx)
        idx_copy.start()
        idx_copy.wait()

        def issue(t, carry):
            for c in range(TOP_K):
                _row_copy(yb_ref, dsm_ref[s, c, t], ybuf_ref.at[s, c], t, sem_rows.at[s]).start(
                    priority=c % DMA_PRIORITIES)
            return carry

        lax.fori_loop(0, bt, issue, 0, unroll=8)

    @pl.when(i == 0)
    def _():
        start_gathers(dest_ref, 0)

    @pl.when(i + 1 < pl.num_programs(0))
    def _():
        start_gathers(dest_next_ref, 1 - slot)

    for c in range(TOP_K):
        pltpu.make_async_copy(yb_ref.at[pl.ds(0, bt * ROW_CHUNKS)], ybuf_ref.at[slot, c],
                              sem_rows.at[slot]).wait()
    route = route_ref[...]
    y0 = _load_token_rows(ybuf_ref, bt, (slot, 0))
    y1 = _load_token_rows(ybuf_ref, bt, (slot, 1))
    out_ref[...] = h_ref[...] + route[:, 2:3] * y0 + route[:, 3:4] * y1


def _combine(h3, route, dest, yb):
    t, d = h3.shape
    bt = min(COMBINE_BLOCK, t)
    n = t // bt
    return pl.pallas_call(
        _combine_kernel,
        out_shape=jax.ShapeDtypeStruct((t, d), F32),
        grid=(n,),
        in_specs=[pl.BlockSpec((bt, d), lambda i: (i, 0)),
                  pl.BlockSpec((bt, ROUTE_COLS), lambda i: (i, 0)),
                  pl.BlockSpec((TOP_K, bt), lambda i: (0, i)),
                  pl.BlockSpec((TOP_K, bt), lambda i: (0, jnp.minimum(i + 1, n - 1))),
                  pl.BlockSpec(memory_space=pl.ANY)],
        out_specs=pl.BlockSpec((bt, d), lambda i: (i, 0)),
        scratch_shapes=[pltpu.SMEM((2, TOP_K, bt), jnp.int32),
                        pltpu.VMEM((2, TOP_K, bt * ROW_CHUNKS, 128), F32),
                        pltpu.SemaphoreType.DMA, pltpu.SemaphoreType.DMA((2,))],
        compiler_params=pltpu.CompilerParams(
            dimension_semantics=("arbitrary",), vmem_limit_bytes=VMEM_LIMIT_BYTES),
        name="moe_combine",
    )(h3, route, dest, dest, yb)


def _moe_layer(h3, hn2, route, wg, wu, wd):
    t, d = h3.shape
    n_tiles = (t * TOP_K) // MOE_TILE + N_EXPERTS
    m_pad = n_tiles * MOE_TILE
    experts = route[:, :TOP_K].astype(jnp.int32).T
    rank, counts = _route_ranks(experts)
    tiles_per = (counts + MOE_TILE - 1) // MOE_TILE
    tile_end = jnp.cumsum(tiles_per)
    start = ((tile_end - tiles_per) * MOE_TILE).astype(jnp.int32)
    tile_ids = jnp.arange(n_tiles, dtype=jnp.int32)
    tile_valid = (tile_ids < tile_end[-1]).astype(jnp.int32)
    last_valid = jnp.maximum(tile_end[-1] - 1, 0)
    tile_expert = jnp.minimum(
        jnp.searchsorted(tile_end, jnp.minimum(tile_ids, last_valid), side="right"),
        N_EXPERTS - 1).astype(jnp.int32)
    has_pad = (counts % MOE_TILE) != 0
    pad_tile = jnp.where(has_pad, tile_end - 1, n_tiles)
    zmask = jnp.zeros((n_tiles + 1,), jnp.int32).at[pad_tile].set(1)[:n_tiles]
    zmask = jnp.maximum(zmask, 1 - tile_valid)
    nz = jnp.sum(zmask).astype(jnp.int32).reshape(1)
    zorder = jnp.argsort(1 - zmask, stable=True).astype(jnp.int32)
    zrows = (zorder[:2 * N_EXPERTS] * MOE_TILE).astype(jnp.int32)
    xs, dest = _dispatch(hn2, experts, rank, start, zrows, nz, m_pad)
    yb = _experts(xs, tile_expert, tile_valid, wg, wu, wd)
    return _combine(h3, route, dest, yb)


def _stages(x, hgrn_w_in, hgrn_lb, hgrn_gnorm, hgrn_w_out, swa_w_q, swa_q_gain, swa_sink, swa_w_o, kv_norm, kv_w, k_gain, rel_bias, attn_norm, ffn_norm, ffn_w_gate, ffn_w_up, ffn_w_down, moe_router, moe_w_gate, moe_w_up, moe_w_down):
    bsz, seq, d = x.shape
    t = bsz * seq
    bf = lambda w: w.astype(BF16)
    h1 = _hgrn_layer(x, attn_norm[0], bf(hgrn_w_in[0]), hgrn_lb, hgrn_gnorm[0], bf(hgrn_w_out[0]))
    h2, q, k2, v2 = _ffn_kvq(h1.reshape(t, d), ffn_norm[0], bf(ffn_w_gate[0]), bf(ffn_w_up[0]),
                             bf(ffn_w_down[0]), kv_norm, kv_w, k_gain, attn_norm[1], bf(swa_w_q[0]))
    nk = SWA_KV_HEADS * SWA_KV_LANES
    h3, hn2, route = _attn_router(q.reshape(bsz, seq, d), k2.reshape(bsz, seq, nk),
                                  v2.reshape(bsz, seq, nk), h2.reshape(bsz, seq, d), rel_bias,
                                  swa_sink[0], swa_q_gain[0], bf(swa_w_o[0]), ffn_norm[1],
                                  moe_router[0])
    h4 = _moe_layer(h3.reshape(t, d), hn2, route.reshape(t, ROUTE_COLS),
                    bf(moe_w_gate[0]), bf(moe_w_up[0]), bf(moe_w_down[0]))
    return {"h1": h1, "h2": h2.reshape(bsz, seq, d), "h3": h3, "h4": h4.reshape(bsz, seq, d)}


def kernel(x, hgrn_w_in, hgrn_lb, hgrn_gnorm, hgrn_w_out, swa_w_q, swa_q_gain, swa_sink, swa_w_o, kv_norm, kv_w, k_gain, rel_bias, attn_norm, ffn_norm, ffn_w_gate, ffn_w_up, ffn_w_down, moe_router, moe_w_gate, moe_w_up, moe_w_down):
    return _stages(x, hgrn_w_in, hgrn_lb, hgrn_gnorm, hgrn_w_out, swa_w_q, swa_q_gain, swa_sink, swa_w_o, kv_norm, kv_w, k_gain, rel_bias, attn_norm, ffn_norm, ffn_w_gate, ffn_w_up, ffn_w_down, moe_router, moe_w_gate, moe_w_up, moe_w_down)["h4"]
```

```python
import functools

import numpy as np
import jax
import jax.numpy as jnp
from jax import lax
from jax.experimental import pallas as pl
from jax.experimental.pallas import tpu as pltpu

F32 = jnp.float32
BF16 = jnp.bfloat16

D_MODEL = 1024
NORM_EPS = 1e-6
LOG2_E = 1.4426950408889634

HG_HEADS = 8
HG_DK = 128
HG_DV = 128
HG_CHUNK = 128
HG_LEVELS = 7
HG_STEP_CHUNKS = 4
HG_GROUP_HEADS = 8

VMEM_LIMIT_BYTES = 56 * 1024 * 1024


def _dot(a, b):
    return jnp.dot(a, b, preferred_element_type=F32)


def _dot_nt(a, b):
    return lax.dot_general(a, b, (((1,), (1,)), ((), ())), preferred_element_type=F32)


def _dot_tn(a, b):
    return lax.dot_general(a, b, (((0,), (0,)), ((), ())), preferred_element_type=F32)


def _rms_scale(x):
    return lax.rsqrt(jnp.mean(x * x, axis=-1, keepdims=True) + NORM_EPS)


def _sigmoid(x):
    return 1.0 / (1.0 + jnp.exp(-x))


ROW_CHUNKS = D_MODEL // 128


def _load_token_rows(ref, n, lead=()):
    return jnp.concatenate(
        [ref[lead + (pl.ds(c, n, stride=ROW_CHUNKS), slice(None))] for c in range(ROW_CHUNKS)], axis=1)


def _store_token_rows(ref, x):
    n = x.shape[0]
    for c in range(ROW_CHUNKS):
        ref[pl.ds(c, n, stride=ROW_CHUNKS), :] = x[:, c * 128:(c + 1) * 128]


def _token_row_slice(ref, token):
    return ref.at[pl.ds(pl.multiple_of(token * ROW_CHUNKS, ROW_CHUNKS), ROW_CHUNKS)]


def _const_spec(shape):
    nd = len(shape)
    return pl.BlockSpec(shape, lambda *_: (0,) * nd, pipeline_mode=pl.Buffered(1))


def _hgrn_level_log_decay(b, level):
    c = HG_CHUNK
    m = 1 << level
    parts = []
    if 2 * m >= 8:
        for start in range(0, c, 2 * m):
            mid = start + m - 1
            parts.append(b[start:start + 2 * m] - b[mid:mid + 1])
    else:
        first_block = lax.broadcasted_iota(jnp.int32, (8, b.shape[1]), 0) < 4
        for start in range(0, c, 8):
            b_mid = jnp.where(first_block, b[start + 1:start + 2], b[start + 5:start + 6])
            parts.append(b[start:start + 8] - b_mid)
    return jnp.concatenate(parts, axis=0)


def _hgrn_kernel(x_ref, gain_ref, win_ref, lbp_ref, tsum_ref, gnorm_ref, wout_ref,
                 out_ref, st_ref, o_ref):
    c = HG_CHUNK

    @pl.when(pl.program_id(1) == 0)
    def _():
        st_ref[...] = jnp.zeros_like(st_ref)

    chunks = [slice(ci * c, (ci + 1) * c) for ci in range(HG_STEP_CHUNKS)]

    lbp = lbp_ref[...]
    lbe = jnp.exp(lbp - jnp.max(lbp, axis=0, keepdims=True))
    lb = lbe[0:1] / jnp.sum(lbe, axis=0, keepdims=True)

    x = x_ref[0]
    hn = (x * _rms_scale(x) * gain_ref[...]).astype(BF16)

    ti = lax.broadcasted_iota(jnp.int32, (c, c), 0)
    si = lax.broadcasted_iota(jnp.int32, (c, c), 1)
    diag = ti == si
    xor = ti ^ si
    lv_mask = [((xor >> l) == 1) & (((ti >> l) & 1) == 1) for l in range(HG_LEVELS)]
    keep = [jnp.where(m, 1.0, 0.0) for m in [diag] + lv_mask]

    row = lax.broadcasted_iota(jnp.int32, (c, HG_DK), 0)
    upper_rows = [((row >> l) & 1) == 1 for l in range(HG_LEVELS)]
    signs = [jnp.where(u, LOG2_E, -LOG2_E) for u in upper_rows]

    gw = HG_GROUP_HEADS * HG_DK
    for p in range(HG_HEADS // HG_GROUP_HEADS):
        cs = slice(p * gw, (p + 1) * gw)
        sec = [_dot(hn, win_ref[:, i * D_MODEL + p * gw:i * D_MODEL + (p + 1) * gw]) for i in range(4)]
        f = lb[:, cs] + (1.0 - lb[:, cs]) * _sigmoid(sec[1])
        g = jnp.log(f)
        g_hi = g.astype(BF16)
        g_lo = (g - g_hi.astype(F32)).astype(BF16)
        b = [_dot(tsum_ref[...], jnp.concatenate([g_hi[rs], g_lo[rs]], axis=0)) for rs in chunks]

        staged = []
        for ci, rs in enumerate(chunks):
            for hh in range(HG_GROUP_HEADS):
                ls = slice(hh * HG_DK, (hh + 1) * HG_DK)
                q_h = sec[0][rs, ls]
                f_h = f[rs, ls]
                b_h = b[ci][:, ls]
                k_h = 1.0 - f_h
                b_last = b_h[c - 1:c]
                per_level = []
                for l in range(HG_LEVELS):
                    if l == 0:
                        w = jnp.where(upper_rows[0], f_h, 1.0)
                    else:
                        w = jnp.exp2(_hgrn_level_log_decay(b_h, l) * signs[l])
                    per_level.append((jnp.where(upper_rows[l], q_h, k_h) * w).astype(BF16))
                staged.append((rs, p * HG_GROUP_HEADS + hh, dict(
                    levels=per_level, q=q_h.astype(BF16), k=k_h.astype(BF16),
                    qb=(q_h * jnp.exp(b_h)).astype(BF16),
                    khat=(k_h * jnp.exp(b_last - b_h)).astype(BF16),
                    decay=jnp.exp(b_last),
                    v=sec[2][rs, ls].astype(BF16),
                    o_gate=sec[3][rs, ls])))

        scores = []
        for _, _, s in staged:
            a = _dot_nt(s["q"], s["k"]) * keep[0]
            for m, lv in zip(keep[1:], s["levels"]):
                a = a + _dot_nt(lv, lv) * m
            scores.append(a.astype(BF16))

        outs = []
        for (_, h, s), a in zip(staged, scores):
            st = st_ref[h]
            outs.append(_dot(a, s["v"]) + _dot_nt(s["qb"], st.astype(BF16)))
            st_ref[h] = st * s["decay"] + _dot_tn(s["v"], s["khat"])

        for (rs, h, s), o_h in zip(staged, outs):
            o_h = o_h * _rms_scale(o_h) * gnorm_ref[...]
            og = s["o_gate"]
            o_ref[rs, h * HG_DV:(h + 1) * HG_DV] = (o_h * (og * _sigmoid(og))).astype(BF16)

    out_ref[0] = x + _dot(o_ref[...], wout_ref[...])


def _hgrn_layer(x, gain, w_in, lb_param, gnorm, w_out):
    bsz, seq, d = x.shape
    c = HG_CHUNK * HG_STEP_CHUNKS
    tril = np.tril(np.ones((HG_CHUNK, HG_CHUNK), np.float32))
    tsum2 = jnp.asarray(np.concatenate([tril, tril], axis=1), dtype=BF16)
    return pl.pallas_call(
        _hgrn_kernel,
        out_shape=jax.ShapeDtypeStruct((bsz, seq, d), F32),
        grid=(bsz, seq // c),
        in_specs=[
            pl.BlockSpec((1, c, d), lambda b, j: (b, j, 0)),
            _const_spec((1, d)),
            _const_spec((d, 4 * d)),
            _const_spec(lb_param.shape),
            _const_spec(tsum2.shape),
            _const_spec((1, HG_DV)),
            _const_spec((d, d)),
        ],
        out_specs=pl.BlockSpec((1, c, d), lambda b, j: (b, j, 0)),
        scratch_shapes=[
            pltpu.VMEM((HG_HEADS, HG_DV, HG_DK), F32),
            pltpu.VMEM((c, d), BF16),
        ],
        compiler_params=pltpu.CompilerParams(
            dimension_semantics=("arbitrary", "arbitrary"),
            vmem_limit_bytes=VMEM_LIMIT_BYTES),
        name="hgrn_layer",
    )(x, gain.reshape(1, d), w_in, lb_param, tsum2, gnorm.reshape(1, HG_DV), w_out)


D_FF_DENSE = 2816
FFN_FF_SPLITS = (0, 768, 1536, 2304, 2816)
FFN_ROWS = 512

SWA_HEAD_DIM = 64
SWA_Q_HEADS = 16
SWA_KV_HEADS = 4
SWA_WINDOW = 128
SWA_KV_LANES = 2 * SWA_HEAD_DIM


def _ffn_kvq_kernel(h_ref, fgain_ref, wg_ref, wu_ref, wd_ref, kvgain_ref, kvw_ref, kgain_ref,
                    qgain_ref, wq_ref, h2_ref, q_ref, k_ref, v_ref):
    h = h_ref[...]
    hn = (h * _rms_scale(h) * fgain_ref[...]).astype(BF16)
    acc = h
    for c in range(len(FFN_FF_SPLITS) - 1):
        sl = slice(FFN_FF_SPLITS[c], FFN_FF_SPLITS[c + 1])
        gate = _dot(hn, wg_ref[:, sl])
        up = _dot(hn, wu_ref[:, sl])
        hid = (gate * _sigmoid(gate) * up).astype(BF16)
        acc = acc + _dot(hid, wd_ref[sl, :])
    h2_ref[...] = acc
    normed = acc * _rms_scale(acc)
    kv = _dot((normed * kvgain_ref[...]).astype(BF16), kvw_ref[...])
    nk = SWA_KV_HEADS * SWA_KV_LANES
    for g in range(SWA_KV_HEADS):
        sl = slice(g * SWA_KV_LANES, (g + 1) * SWA_KV_LANES)
        kg = kv[:, sl]
        k_ref[:, sl] = (kg * _rms_scale(kg) * kgain_ref[...]).astype(BF16)
    v_ref[...] = kv[:, nk:].astype(BF16)
    q_ref[...] = _dot((normed * qgain_ref[...]).astype(BF16), wq_ref[...]).astype(BF16)


def _dup_kv_columns(w):
    d = w.shape[0]
    w = w.reshape(d, SWA_KV_HEADS, 1, SWA_HEAD_DIM)
    return jnp.broadcast_to(w, (d, SWA_KV_HEADS, 2, SWA_HEAD_DIM)).reshape(d, SWA_KV_HEADS * SWA_KV_LANES)


def _ffn_kvq(h1, fgain, wg, wu, wd, kvgain, kv_w, k_gain, qgain, wq):
    t, d = h1.shape
    rows = min(FFN_ROWS, t)
    nkv = SWA_KV_HEADS * SWA_HEAD_DIM
    kvw2 = jnp.concatenate([_dup_kv_columns(kv_w[:, :nkv]), _dup_kv_columns(kv_w[:, nkv:])],
                           axis=1).astype(BF16)
    kgain2 = jnp.concatenate([k_gain, k_gain]).reshape(1, SWA_KV_LANES)
    nk = SWA_KV_HEADS * SWA_KV_LANES
    row_spec = lambda w: pl.BlockSpec((rows, w), lambda i: (i, 0))
    return pl.pallas_call(
        _ffn_kvq_kernel,
        out_shape=(jax.ShapeDtypeStruct((t, d), F32), jax.ShapeDtypeStruct((t, d), BF16),
                   jax.ShapeDtypeStruct((t, nk), BF16), jax.ShapeDtypeStruct((t, nk), BF16)),
        grid=(t // rows,),
        in_specs=[
            row_spec(d),
            _const_spec((1, d)),
            _const_spec((d, D_FF_DENSE)), _const_spec((d, D_FF_DENSE)), _const_spec((D_FF_DENSE, d)),
            _const_spec((1, d)), _const_spec((d, 2 * nk)), _const_spec((1, SWA_KV_LANES)),
            _const_spec((1, d)), _const_spec((d, d)),
        ],
        out_specs=(row_spec(d), row_spec(d), row_spec(nk), row_spec(nk)),
        compiler_params=pltpu.CompilerParams(
            dimension_semantics=("arbitrary",), vmem_limit_bytes=VMEM_LIMIT_BYTES),
        name="ffn_kvq",
    )(h1, fgain.reshape(1, d), wg, wu, wd, kvgain.reshape(1, d), kvw2, kgain2,
      qgain.reshape(1, d), wq)


ATTN_ROWS = 256
REL_BUCKETS = 32
REL_MAX_DIST = 128
N_EXPERTS = 8
ROUTE_LANES = 128
ROUTE_COLS = 8


def _rel_bucket_table():
    w = SWA_WINDOW
    qi = np.arange(w)[:, None]
    kj = np.arange(2 * w)[None, :]
    dist = qi + w - kj
    in_win = (dist >= 0) & (dist < w)
    n = np.maximum(dist, 0)
    max_exact = REL_BUCKETS // 2
    nf = np.maximum(n, 1).astype(np.float64)
    large = max_exact + (np.log(nf / max_exact) / np.log(REL_MAX_DIST / max_exact)
                         * (REL_BUCKETS - max_exact)).astype(np.int64)
    large = np.minimum(large, REL_BUCKETS - 1)
    bucket = np.where(n < max_exact, n, large)
    return np.where(in_win, bucket, -1).astype(np.int32)


def _attn_router_kernel(relb_ref, sink_ref, q_ref, kc_ref, kp_ref, vc_ref, vp_ref, h2_ref,
                        bucket_ref, qgain_ref, wo_ref, fgain_ref, wr_ref,
                        h3_ref, hn2_ref, route_ref, bias_ref, attn_ref):
    w = SWA_WINDOW
    hd = SWA_HEAD_DIM
    rows = q_ref.shape[1]
    first_tile = pl.program_id(1) == 0

    @pl.when(jnp.logical_and(pl.program_id(0) == 0, first_tile))
    def _():
        bucket = bucket_ref[...]
        for h in range(SWA_Q_HEADS):
            plane = jnp.full((w, 2 * w), -jnp.inf, F32)
            for bkt in range(REL_BUCKETS):
                plane = jnp.where(bucket == bkt, relb_ref[bkt * SWA_Q_HEADS + h], plane)
            bias_ref[h] = plane

    lane = lax.broadcasted_iota(jnp.int32, (2 * w, SWA_KV_LANES), 1)
    left = lane < hd
    lane_q = lax.broadcasted_iota(jnp.int32, (w, SWA_KV_LANES), 1)
    left_q = lane_q < hd
    key_idx = lax.broadcasted_iota(jnp.int32, (1, 2 * w), 1)
    qgain = qgain_ref[...]

    k_all = jnp.concatenate([kp_ref[0], kc_ref[0]], axis=0)
    v_all = jnp.concatenate([vp_ref[0], vc_ref[0]], axis=0)
    zero = jnp.zeros((), BF16)
    no_prev = jnp.where(jnp.logical_and(first_tile, key_idx < w), -jnp.inf, 0.0)
    units = [(blk, g) for blk in range(rows // w) for g in range(SWA_KV_HEADS)]

    score, values = [], []
    for blk, g in units:
        r0 = blk * w
        ks = slice(g * SWA_KV_LANES, (g + 1) * SWA_KV_LANES)
        k2 = k_all[r0:r0 + 2 * w, ks]
        v2 = v_all[r0:r0 + 2 * w, ks]
        k_bd = jnp.concatenate([jnp.where(left, k2, zero), jnp.where(left, zero, k2)], axis=0)
        values.append(jnp.concatenate([jnp.where(left, v2, zero), jnp.where(left, zero, v2)], axis=0))
        pairs = []
        for p in range(2):
            c0 = (4 * g + 2 * p) * hd
            qp = q_ref[0, r0:r0 + w, c0:c0 + 2 * hd].astype(F32)
            sq = qp * qp
            s_l = jnp.sum(jnp.where(left_q, sq, 0.0), axis=-1, keepdims=True)
            s_r = jnp.sum(jnp.where(left_q, 0.0, sq), axis=-1, keepdims=True)
            ms = jnp.where(left_q, s_l, s_r) * (1.0 / hd)
            pairs.append((qp * lax.rsqrt(ms + NORM_EPS) * qgain * (hd ** -0.5)).astype(BF16))
        qs = jnp.concatenate(pairs, axis=0)
        score.append(_dot_nt(qs, k_bd))

    probs, sinks = [], []
    for (blk, g), s in zip(units, score):
        e_rows, sink_rows = [], []
        for p in range(2):
            halves = []
            for side in range(2):
                hq = 4 * g + 2 * p + side
                logit = s[p * w:(p + 1) * w, side * 2 * w:(side + 1) * 2 * w] + bias_ref[hq]
                if blk == 0:
                    logit = logit + no_prev
                sink = sink_ref[hq]
                m = jnp.maximum(jnp.max(logit, axis=-1, keepdims=True), sink)
                halves.append((jnp.exp(logit - m), jnp.exp(sink - m)))
            e_rows.append(jnp.concatenate([halves[0][0], halves[1][0]], axis=1).astype(BF16))
            sink_rows.append(jnp.where(left_q, halves[0][1], halves[1][1]))
        probs.append(jnp.concatenate(e_rows, axis=0))
        sinks.append(jnp.concatenate(sink_rows, axis=0))

    ones_bd = jnp.concatenate([jnp.where(left, 1.0, 0.0), jnp.where(left, 0.0, 1.0)], axis=0).astype(BF16)
    for (blk, g), e2, v_bd, sink_term in zip(units, probs, values, sinks):
        pvd = _dot(e2, jnp.concatenate([v_bd, ones_bd], axis=1))
        pv = pvd[:, :SWA_KV_LANES] / (pvd[:, SWA_KV_LANES:] + sink_term)
        r0 = blk * w
        for p in range(2):
            c0 = (4 * g + 2 * p) * hd
            attn_ref[r0:r0 + w, c0:c0 + 2 * hd] = pv[p * w:(p + 1) * w].astype(BF16)

    h3 = h2_ref[0] + _dot(attn_ref[...], wo_ref[...])
    h3_ref[0] = h3
    hn2 = h3 * _rms_scale(h3) * fgain_ref[...]
    _store_token_rows(hn2_ref, hn2)

    hi = hn2.astype(BF16)
    lo = (hn2 - hi.astype(F32)).astype(BF16)
    logits = _dot(hi, wr_ref[0]) + _dot(lo, wr_ref[0]) + _dot(hi, wr_ref[1])
    col = lax.broadcasted_iota(jnp.int32, logits.shape, 1)
    logits = jnp.where(col < N_EXPERTS, logits, -jnp.inf)
    m0 = jnp.max(logits, axis=-1, keepdims=True)
    i0 = jnp.min(jnp.where(logits == m0, col, ROUTE_LANES), axis=-1, keepdims=True)
    rest = jnp.where(col == i0, -jnp.inf, logits)
    m1 = jnp.max(rest, axis=-1, keepdims=True)
    i1 = jnp.min(jnp.where(rest == m1, col, ROUTE_LANES), axis=-1, keepdims=True)
    e1 = jnp.exp(m1 - m0)
    g0 = 1.0 / (1.0 + e1)
    g1 = e1 / (1.0 + e1)
    rc = lax.broadcasted_iota(jnp.int32, (rows, ROUTE_COLS), 1)
    rec = jnp.where(rc == 0, i0.astype(F32),
                    jnp.where(rc == 1, i1.astype(F32),
                              jnp.where(rc == 2, g0, jnp.where(rc == 3, g1, 0.0))))
    route_ref[0] = rec


def _attn_router(q, k2, v2, h2, rel_bias, sink, q_gain, w_o, fgain, w_router):
    bsz, seq, d = h2.shape
    rows = min(ATTN_ROWS, seq)
    w = SWA_WINDOW
    nk = SWA_KV_HEADS * SWA_KV_LANES
    per = rows // w
    bucket = jnp.asarray(_rel_bucket_table())
    qgain2 = jnp.concatenate([q_gain, q_gain]).reshape(1, SWA_KV_LANES)
    wr = jnp.zeros((d, ROUTE_LANES), F32).at[:, :N_EXPERTS].set(w_router)
    wr_hi = wr.astype(BF16)
    wr_lo = (wr - wr_hi.astype(F32)).astype(BF16)
    wr2 = jnp.stack([wr_hi, wr_lo])
    tile = lambda width: pl.BlockSpec((1, rows, width), lambda b, j, *_: (b, j, 0))
    prev = pl.BlockSpec((1, w, nk), lambda b, j, *_: (b, jnp.maximum(j * per - 1, 0), 0))
    const = lambda shape: pl.BlockSpec(shape, lambda b, j, *_: (0,) * len(shape),
                                       pipeline_mode=pl.Buffered(1))
    return pl.pallas_call(
        _attn_router_kernel,
        out_shape=(jax.ShapeDtypeStruct((bsz, seq, d), F32),
                   jax.ShapeDtypeStruct((bsz * seq * ROW_CHUNKS, 128), F32),
                   jax.ShapeDtypeStruct((bsz, seq, ROUTE_COLS), F32)),
        grid_spec=pltpu.PrefetchScalarGridSpec(
            num_scalar_prefetch=2,
            grid=(bsz, seq // rows),
            in_specs=[tile(d), tile(nk), prev, tile(nk), prev, tile(d),
                      const((w, 2 * w)), const((1, SWA_KV_LANES)), const((d, d)), const((1, d)),
                      const((2, d, ROUTE_LANES))],
            out_specs=(tile(d),
                       pl.BlockSpec((rows * ROW_CHUNKS, 128), lambda b, j, *_: (b * (seq // rows) + j, 0)),
                       tile(ROUTE_COLS)),
            scratch_shapes=[pltpu.VMEM((SWA_Q_HEADS, w, 2 * w), F32), pltpu.VMEM((rows, d), BF16)],
        ),
        compiler_params=pltpu.CompilerParams(
            dimension_semantics=("arbitrary", "arbitrary"), vmem_limit_bytes=VMEM_LIMIT_BYTES),
        name="attn_router",
    )(rel_bias.reshape(-1), sink, q, k2, k2, v2, v2, h2, bucket, qgain2, w_o, fgain.reshape(1, d), wr2)


TOP_K = 2
D_FF_EXPERT = 3584
MOE_TILE = 512
MOE_FF_SUB = 512
RANK_BLOCK = 1024
DISPATCH_BLOCK = 1024
COMBINE_BLOCK = 512
ZERO_ROWS = 256
RANK_SUBLANES = 16
DMA_PRIORITIES = 2


def _rank_kernel(e_ref, tri_ref, rank_ref, cnt_ref, carry_ref):
    bt = e_ref.shape[1]

    @pl.when(pl.program_id(0) == 0)
    def _():
        carry_ref[...] = jnp.zeros_like(carry_ref)

    ex = lax.broadcasted_iota(jnp.int32, (RANK_SUBLANES, bt), 0)
    oh0 = ex == e_ref[0:1, :]
    oh1 = ex == e_ref[1:2, :]
    both = oh0.astype(F32) + oh1.astype(F32)
    pos = _dot(both.astype(BF16), tri_ref[...]) + carry_ref[:, 0:1]
    rank_ref[0:1, :] = jnp.sum(jnp.where(oh0, pos, 0.0), axis=0, keepdims=True).astype(jnp.int32)
    rank_ref[1:2, :] = jnp.sum(jnp.where(oh1, pos, 0.0), axis=0, keepdims=True).astype(jnp.int32)
    carry_ref[...] = carry_ref[...] + jnp.sum(both, axis=1, keepdims=True)
    cnt_ref[...] = carry_ref[...]


def _route_ranks(experts):
    _, t = experts.shape
    bt = min(RANK_BLOCK, t)
    tri = jnp.asarray(np.triu(np.ones((bt, bt), np.float32), k=1), dtype=BF16)
    rank, cnt = pl.pallas_call(
        _rank_kernel,
        out_shape=(jax.ShapeDtypeStruct((TOP_K, t), jnp.int32),
                   jax.ShapeDtypeStruct((RANK_SUBLANES, 128), F32)),
        grid=(t // bt,),
        in_specs=[pl.BlockSpec((TOP_K, bt), lambda i: (0, i)), _const_spec((bt, bt))],
        out_specs=(pl.BlockSpec((TOP_K, bt), lambda i: (0, i)),
                   pl.BlockSpec((RANK_SUBLANES, 128), lambda i: (0, 0))),
        scratch_shapes=[pltpu.VMEM((RANK_SUBLANES, 128), F32)],
        compiler_params=pltpu.CompilerParams(dimension_semantics=("arbitrary",)),
        name="route_rank",
    )(experts, tri)
    return rank, cnt[:N_EXPERTS, 0].astype(jnp.int32)


def _row_copy(src, src_token, dst, dst_token, sem):
    return pltpu.make_async_copy(_token_row_slice(src, src_token), _token_row_slice(dst, dst_token), sem)


def _dispatch_kernel(start_ref, zrow_ref, nz_ref, hn_ref, e_ref, rank_ref, xs_ref, dest_ref,
                     dvm_ref, dsm_ref, zero_ref, sem_idx, sem_rows, sem_zero):
    bt = e_ref.shape[1]
    zero_copies = MOE_TILE // ZERO_ROWS

    def zero_copy(z, c):
        row = pl.multiple_of((zrow_ref[z] + c * ZERO_ROWS) * ROW_CHUNKS, ROW_CHUNKS)
        return pltpu.make_async_copy(zero_ref, xs_ref.at[pl.ds(row, ZERO_ROWS * ROW_CHUNKS)], sem_zero)

    @pl.when(pl.program_id(0) == 0)
    def _():
        zero_ref[...] = jnp.zeros_like(zero_ref)

        def start(z, carry):
            for c in range(zero_copies):
                zero_copy(z, c).start()
            return carry

        def wait(z, carry):
            for c in range(zero_copies):
                zero_copy(z, c).wait()
            return carry

        lax.fori_loop(0, nz_ref[0], start, 0)
        lax.fori_loop(0, nz_ref[0], wait, 0)

    e = e_ref[...]
    base = jnp.zeros_like(e)
    for x in range(N_EXPERTS):
        base = jnp.where(e == x, start_ref[x], base)
    dest = base + rank_ref[...]
    dest_ref[...] = dest
    dvm_ref[...] = dest
    idx_copy = pltpu.make_async_copy(dvm_ref, dsm_ref, sem_idx)
    idx_copy.start()
    idx_copy.wait()

    def issue(t, carry):
        for c in range(TOP_K):
            _row_copy(hn_ref, t, xs_ref, dsm_ref[c, t], sem_rows).start(priority=c % DMA_PRIORITIES)
        return carry

    lax.fori_loop(0, bt, issue, 0, unroll=8)
    for c in range(TOP_K):
        pltpu.make_async_copy(hn_ref, xs_ref.at[pl.ds(0, bt * ROW_CHUNKS)], sem_rows).wait()


def _dispatch(hn, experts, rank, start, zrows, nz, m_pad):
    t = hn.shape[0] // ROW_CHUNKS
    bt = min(DISPATCH_BLOCK, t)
    blk = lambda i, *_: (0, i)
    return pl.pallas_call(
        _dispatch_kernel,
        out_shape=(jax.ShapeDtypeStruct((m_pad * ROW_CHUNKS, 128), F32),
                   jax.ShapeDtypeStruct((TOP_K, t), jnp.int32)),
        grid_spec=pltpu.PrefetchScalarGridSpec(
            num_scalar_prefetch=3,
            grid=(t // bt,),
            in_specs=[pl.BlockSpec((bt * ROW_CHUNKS, 128), lambda i, *_: (i, 0)),
                      pl.BlockSpec((TOP_K, bt), blk), pl.BlockSpec((TOP_K, bt), blk)],
            out_specs=(pl.BlockSpec(memory_space=pl.ANY), pl.BlockSpec((TOP_K, bt), blk)),
            scratch_shapes=[pltpu.VMEM((TOP_K, bt), jnp.int32), pltpu.SMEM((TOP_K, bt), jnp.int32),
                            pltpu.VMEM((ZERO_ROWS * ROW_CHUNKS, 128), F32),
                            pltpu.SemaphoreType.DMA, pltpu.SemaphoreType.DMA, pltpu.SemaphoreType.DMA],
        ),
        compiler_params=pltpu.CompilerParams(dimension_semantics=("arbitrary",)),
        name="moe_dispatch",
    )(start, zrows, nz, hn, experts, rank)


def _expert_kernel(te_ref, tv_ref, x_ref, wg_ref, wu_ref, wd_ref, out_ref):
    valid = tv_ref[pl.program_id(0)] == 1

    @pl.when(valid)
    def _():
        xb = _load_token_rows(x_ref, MOE_TILE).astype(BF16)
        acc = jnp.zeros((MOE_TILE, D_MODEL), F32)
        for j in range(D_FF_EXPERT // MOE_FF_SUB):
            sl = slice(j * MOE_FF_SUB, (j + 1) * MOE_FF_SUB)
            gate = _dot(xb, wg_ref[0, :, sl])
            up = _dot(xb, wu_ref[0, :, sl])
            hid = (gate * _sigmoid(gate) * up).astype(BF16)
            acc = acc + _dot(hid, wd_ref[0, sl, :])
        _store_token_rows(out_ref, acc)

    @pl.when(jnp.logical_not(valid))
    def _():
        out_ref[...] = jnp.zeros_like(out_ref)


def _experts(xs, tile_expert, tile_valid, wg, wu, wd):
    d = D_MODEL
    row_block = pl.BlockSpec((MOE_TILE * ROW_CHUNKS, 128), lambda i, te, tv: (i, 0))
    weights = lambda shape: pl.BlockSpec((1,) + shape, lambda i, te, tv: (te[i], 0, 0),
                                         pipeline_mode=pl.Buffered(1))
    return pl.pallas_call(
        _expert_kernel,
        out_shape=jax.ShapeDtypeStruct(xs.shape, F32),
        grid_spec=pltpu.PrefetchScalarGridSpec(
            num_scalar_prefetch=2,
            grid=(xs.shape[0] // (MOE_TILE * ROW_CHUNKS),),
            in_specs=[row_block, weights((d, D_FF_EXPERT)), weights((d, D_FF_EXPERT)),
                      weights((D_FF_EXPERT, d))],
            out_specs=row_block,
        ),
        compiler_params=pltpu.CompilerParams(
            dimension_semantics=("arbitrary",), vmem_limit_bytes=VMEM_LIMIT_BYTES),
        name="moe_experts",
    )(tile_expert, tile_valid, xs, wg, wu, wd)


def _combine_kernel(h_ref, route_ref, dest_ref, dest_next_ref, yb_ref, out_ref,
                    dsm_ref, ybuf_ref, sem_idx, sem_rows):
    bt = h_ref.shape[0]
    i = pl.program_id(0)
    slot = i % 2

    def start_gathers(step_dest_ref, s):
        idx_copy = pltpu.make_async_copy(step_dest_ref, dsm_ref.at[s], sem_idx)
        idx_copy.start()
        idx_copy.wait()

        def issue(t, carry):
            for c in range(TOP_K):
                _row_copy(yb_ref, dsm_ref[s, c, t], ybuf_ref.at[s, c], t, sem_rows.at[s]).start(
                    priority=c % DMA_PRIORITIES)
            return carry

        lax.fori_loop(0, bt, issue, 0, unroll=8)

    @pl.when(i == 0)
    def _():
        start_gathers(dest_ref, 0)

    @pl.when(i + 1 < pl.num_programs(0))
    def _():
        start_gathers(dest_next_ref, 1 - slot)

    for c in range(TOP_K):
        pltpu.make_async_copy(yb_ref.at[pl.ds(0, bt * ROW_CHUNKS)], ybuf_ref.at[slot, c],
                              sem_rows.at[slot]).wait()
    route = route_ref[...]
    y0 = _load_token_rows(ybuf_ref, bt, (slot, 0))
    y1 = _load_token_rows(ybuf_ref, bt, (slot, 1))
    out_ref[...] = h_ref[...] + route[:, 2:3] * y0 + route[:, 3:4] * y1


def _combine(h3, route, dest, yb):
    t, d = h3.shape
    bt = min(COMBINE_BLOCK, t)
    n = t // bt
    return pl.pallas_call(
        _combine_kernel,
        out_shape=jax.ShapeDtypeStruct((t, d), F32),
        grid=(n,),
        in_specs=[pl.BlockSpec((bt, d), lambda i: (i, 0)),
                  pl.BlockSpec((bt, ROUTE_COLS), lambda i: (i, 0)),
                  pl.BlockSpec((TOP_K, bt), lambda i: (0, i)),
                  pl.BlockSpec((TOP_K, bt), lambda i: (0, jnp.minimum(i + 1, n - 1))),
                  pl.BlockSpec(memory_space=pl.ANY)],
        out_specs=pl.BlockSpec((bt, d), lambda i: (i, 0)),
        scratch_shapes=[pltpu.SMEM((2, TOP_K, bt), jnp.int32),
                        pltpu.VMEM((2, TOP_K, bt * ROW_CHUNKS, 128), F32),
                        pltpu.SemaphoreType.DMA, pltpu.SemaphoreType.DMA((2,))],
        compiler_params=pltpu.CompilerParams(
            dimension_semantics=("arbitrary",), vmem_limit_bytes=VMEM_LIMIT_BYTES),
        name="moe_combine",
    )(h3, route, dest, dest, yb)


def _moe_layer(h3, hn2, route, wg, wu, wd):
    t, d = h3.shape
    n_tiles = (t * TOP_K) // MOE_TILE + N_EXPERTS
    m_pad = n_tiles * MOE_TILE
    experts = route[:, :TOP_K].astype(jnp.int32).T
    rank, counts = _route_ranks(experts)
    tiles_per = (counts + MOE_TILE - 1) // MOE_TILE
    tile_end = jnp.cumsum(tiles_per)
    start = ((tile_end - tiles_per) * MOE_TILE).astype(jnp.int32)
    tile_ids = jnp.arange(n_tiles, dtype=jnp.int32)
    tile_valid = (tile_ids < tile_end[-1]).astype(jnp.int32)
    last_valid = jnp.maximum(tile_end[-1] - 1, 0)
    tile_expert = jnp.minimum(
        jnp.sum(jnp.minimum(tile_ids, last_valid)[:, None] >= tile_end[None, :], axis=1),
        N_EXPERTS - 1).astype(jnp.int32)
    has_pad = (counts % MOE_TILE) != 0
    pad_tile = jnp.where(has_pad, tile_end - 1, n_tiles)
    zmask = jnp.zeros((n_tiles + 1,), jnp.int32).at[pad_tile].set(1)[:n_tiles]
    zmask = jnp.maximum(zmask, 1 - tile_valid)
    nz = jnp.sum(zmask).astype(jnp.int32).reshape(1)
    zorder = jnp.argsort(1 - zmask, stable=True).astype(jnp.int32)
    zrows = (zorder[:2 * N_EXPERTS] * MOE_TILE).astype(jnp.int32)
    xs, dest = _dispatch(hn2, experts, rank, start, zrows, nz, m_pad)
    yb = _experts(xs, tile_expert, tile_valid, wg, wu, wd)
    return _combine(h3, route, dest, yb)


def _stages(x, hgrn_w_in, hgrn_lb, hgrn_gnorm, hgrn_w_out, swa_w_q, swa_q_gain, swa_sink, swa_w_o, kv_norm, kv_w, k_gain, rel_bias, attn_norm, ffn_norm, ffn_w_gate, ffn_w_up, ffn_w_down, moe_router, moe_w_gate, moe_w_up, moe_w_down):
    bsz, seq, d = x.shape
    t = bsz * seq
    bf = lambda w: w.astype(BF16)
    h1 = _hgrn_layer(x, attn_norm[0], bf(hgrn_w_in[0]), hgrn_lb, hgrn_gnorm[0], bf(hgrn_w_out[0]))
    h2, q, k2, v2 = _ffn_kvq(h1.reshape(t, d), ffn_norm[0], bf(ffn_w_gate[0]), bf(ffn_w_up[0]),
                             bf(ffn_w_down[0]), kv_norm, kv_w, k_gain, attn_norm[1], bf(swa_w_q[0]))
    nk = SWA_KV_HEADS * SWA_KV_LANES
    h3, hn2, route = _attn_router(q.reshape(bsz, seq, d), k2.reshape(bsz, seq, nk),
                                  v2.reshape(bsz, seq, nk), h2.reshape(bsz, seq, d), rel_bias,
                                  swa_sink[0], swa_q_gain[0], bf(swa_w_o[0]), ffn_norm[1],
                                  moe_router[0])
    h4 = _moe_layer(h3.reshape(t, d), hn2, route.reshape(t, ROUTE_COLS),
                    bf(moe_w_gate[0]), bf(moe_w_up[0]), bf(moe_w_down[0]))
    return {"h1": h1, "h2": h2.reshape(bsz, seq, d), "h3": h3, "h4": h4.reshape(bsz, seq, d)}


def kernel(x, hgrn_w_in, hgrn_lb, hgrn_gnorm, hgrn_w_out, swa_w_q, swa_q_gain, swa_sink, swa_w_o, kv_norm, kv_w, k_gain, rel_bias, attn_norm, ffn_norm, ffn_w_gate, ffn_w_up, ffn_w_down, moe_router, moe_w_gate, moe_w_up, moe_w_down):
    return _stages(x, hgrn_w_in, hgrn_lb, hgrn_gnorm, hgrn_w_out, swa_w_q, swa_q_gain, swa_sink, swa_w_o, kv_norm, kv_w, k_gain, rel_bias, attn_norm, ffn_norm, ffn_w_gate, ffn_w_up, ffn_w_down, moe_router, moe_w_gate, moe_w_up, moe_w_down)["h4"]
```

```python
import functools

import numpy as np
import jax
import jax.numpy as jnp
from jax import lax
from jax.experimental import pallas as pl
from jax.experimental.pallas import tpu as pltpu

F32 = jnp.float32
BF16 = jnp.bfloat16

D_MODEL = 1024
NORM_EPS = 1e-6
LOG2_E = 1.4426950408889634

HG_HEADS = 8
HG_DK = 128
HG_DV = 128
HG_CHUNK = 128
HG_LEVELS = 7
HG_STEP_CHUNKS = 4
HG_GROUP_HEADS = 8

VMEM_LIMIT_BYTES = 56 * 1024 * 1024


def _dot(a, b):
    return jnp.dot(a, b, preferred_element_type=F32)


def _dot_nt(a, b):
    return lax.dot_general(a, b, (((1,), (1,)), ((), ())), preferred_element_type=F32)


def _dot_tn(a, b):
    return lax.dot_general(a, b, (((0,), (0,)), ((), ())), preferred_element_type=F32)


def _rms_scale(x):
    return lax.rsqrt(jnp.mean(x * x, axis=-1, keepdims=True) + NORM_EPS)


def _sigmoid(x):
    return 1.0 / (1.0 + jnp.exp(-x))


ROW_CHUNKS = D_MODEL // 128


def _load_token_rows(ref, n, lead=()):
    return jnp.concatenate(
        [ref[lead + (pl.ds(c, n, stride=ROW_CHUNKS), slice(None))] for c in range(ROW_CHUNKS)], axis=1)


def _store_token_rows(ref, x):
    n = x.shape[0]
    for c in range(ROW_CHUNKS):
        ref[pl.ds(c, n, stride=ROW_CHUNKS), :] = x[:, c * 128:(c + 1) * 128]


def _token_row_slice(ref, token):
    return ref.at[pl.ds(pl.multiple_of(token * ROW_CHUNKS, ROW_CHUNKS), ROW_CHUNKS)]


def _const_spec(shape):
    nd = len(shape)
    return pl.BlockSpec(shape, lambda *_: (0,) * nd, pipeline_mode=pl.Buffered(1))


def _hgrn_level_log_decay(b, level):
    c = HG_CHUNK
    m = 1 << level
    parts = []
    if 2 * m >= 8:
        for start in range(0, c, 2 * m):
            mid = start + m - 1
            parts.append(b[start:start + 2 * m] - b[mid:mid + 1])
    else:
        first_block = lax.broadcasted_iota(jnp.int32, (8, b.shape[1]), 0) < 4
        for start in range(0, c, 8):
            b_mid = jnp.where(first_block, b[start + 1:start + 2], b[start + 5:start + 6])
            parts.append(b[start:start + 8] - b_mid)
    return jnp.concatenate(parts, axis=0)


def _hgrn_kernel(x_ref, gain_ref, win_ref, lbp_ref, tsum_ref, gnorm_ref, wout_ref,
                 out_ref, st_ref, o_ref):
    c = HG_CHUNK

    @pl.when(pl.program_id(1) == 0)
    def _():
        st_ref[...] = jnp.zeros_like(st_ref)

    chunks = [slice(ci * c, (ci + 1) * c) for ci in range(HG_STEP_CHUNKS)]

    lbp = lbp_ref[...]
    lbe = jnp.exp(lbp - jnp.max(lbp, axis=0, keepdims=True))
    lb = lbe[0:1] / jnp.sum(lbe, axis=0, keepdims=True)

    x = x_ref[0]
    hn = (x * _rms_scale(x) * gain_ref[...]).astype(BF16)

    ti = lax.broadcasted_iota(jnp.int32, (c, c), 0)
    si = lax.broadcasted_iota(jnp.int32, (c, c), 1)
    diag = ti == si
    xor = ti ^ si
    lv_mask = [((xor >> l) == 1) & (((ti >> l) & 1) == 1) for l in range(HG_LEVELS)]
    keep = [jnp.where(m, 1.0, 0.0) for m in [diag] + lv_mask]

    row = lax.broadcasted_iota(jnp.int32, (c, HG_DK), 0)
    upper_rows = [((row >> l) & 1) == 1 for l in range(HG_LEVELS)]
    signs = [jnp.where(u, LOG2_E, -LOG2_E) for u in upper_rows]

    gw = HG_GROUP_HEADS * HG_DK
    for p in range(HG_HEADS // HG_GROUP_HEADS):
        cs = slice(p * gw, (p + 1) * gw)
        sec = [_dot(hn, win_ref[:, i * D_MODEL + p * gw:i * D_MODEL + (p + 1) * gw]) for i in range(4)]
        f = lb[:, cs] + (1.0 - lb[:, cs]) * _sigmoid(sec[1])
        g = jnp.log(f)
        g_hi = g.astype(BF16)
        g_lo = (g - g_hi.astype(F32)).astype(BF16)
        b = [_dot(tsum_ref[...], jnp.concatenate([g_hi[rs], g_lo[rs]], axis=0)) for rs in chunks]

        staged = []
        for ci, rs in enumerate(chunks):
            for hh in range(HG_GROUP_HEADS):
                ls = slice(hh * HG_DK, (hh + 1) * HG_DK)
                q_h = sec[0][rs, ls]
                f_h = f[rs, ls]
                b_h = b[ci][:, ls]
                k_h = 1.0 - f_h
                b_last = b_h[c - 1:c]
                per_level = []
                for l in range(HG_LEVELS):
                    if l == 0:
                        w = jnp.where(upper_rows[0], f_h, 1.0)
                    else:
                        w = jnp.exp2(_hgrn_level_log_decay(b_h, l) * signs[l])
                    per_level.append((jnp.where(upper_rows[l], q_h, k_h) * w).astype(BF16))
                staged.append((rs, p * HG_GROUP_HEADS + hh, dict(
                    levels=per_level, q=q_h.astype(BF16), k=k_h.astype(BF16),
                    qb=(q_h * jnp.exp(b_h)).astype(BF16),
                    khat=(k_h * jnp.exp(b_last - b_h)).astype(BF16),
                    decay=jnp.exp(b_last),
                    v=sec[2][rs, ls].astype(BF16),
                    o_gate=sec[3][rs, ls])))

        scores = []
        for _, _, s in staged:
            a = _dot_nt(s["q"], s["k"]) * keep[0]
            for m, lv in zip(keep[1:], s["levels"]):
                a = a + _dot_nt(lv, lv) * m
            scores.append(a.astype(BF16))

        outs = []
        for (_, h, s), a in zip(staged, scores):
            st = st_ref[h]
            outs.append(_dot(a, s["v"]) + _dot_nt(s["qb"], st.astype(BF16)))
            st_ref[h] = st * s["decay"] + _dot_tn(s["v"], s["khat"])

        for (rs, h, s), o_h in zip(staged, outs):
            o_h = o_h * _rms_scale(o_h) * gnorm_ref[...]
            og = s["o_gate"]
            o_ref[rs, h * HG_DV:(h + 1) * HG_DV] = (o_h * (og * _sigmoid(og))).astype(BF16)

    out_ref[0] = x + _dot(o_ref[...], wout_ref[...])


def _hgrn_layer(x, gain, w_in, lb_param, gnorm, w_out):
    bsz, seq, d = x.shape
    c = HG_CHUNK * HG_STEP_CHUNKS
    tril = np.tril(np.ones((HG_CHUNK, HG_CHUNK), np.float32))
    tsum2 = jnp.asarray(np.concatenate([tril, tril], axis=1), dtype=BF16)
    return pl.pallas_call(
        _hgrn_kernel,
        out_shape=jax.ShapeDtypeStruct((bsz, seq, d), F32),
        grid=(bsz, seq // c),
        in_specs=[
            pl.BlockSpec((1, c, d), lambda b, j: (b, j, 0)),
            _const_spec((1, d)),
            _const_spec((d, 4 * d)),
            _const_spec(lb_param.shape),
            _const_spec(tsum2.shape),
            _const_spec((1, HG_DV)),
            _const_spec((d, d)),
        ],
        out_specs=pl.BlockSpec((1, c, d), lambda b, j: (b, j, 0)),
        scratch_shapes=[
            pltpu.VMEM((HG_HEADS, HG_DV, HG_DK), F32),
            pltpu.VMEM((c, d), BF16),
        ],
        compiler_params=pltpu.CompilerParams(
            dimension_semantics=("arbitrary", "arbitrary"),
            vmem_limit_bytes=VMEM_LIMIT_BYTES),
        name="hgrn_layer",
    )(x, gain.reshape(1, d), w_in, lb_param, tsum2, gnorm.reshape(1, HG_DV), w_out)


D_FF_DENSE = 2816
FFN_FF_SPLITS = (0, 768, 1536, 2304, 2816)
FFN_ROWS = 512

SWA_HEAD_DIM = 64
SWA_Q_HEADS = 16
SWA_KV_HEADS = 4
SWA_WINDOW = 128
SWA_KV_LANES = 2 * SWA_HEAD_DIM


def _ffn_kvq_kernel(h_ref, fgain_ref, wg_ref, wu_ref, wd_ref, kvgain_ref, kvw_ref, kgain_ref,
                    qgain_ref, wq_ref, h2_ref, q_ref, k_ref, v_ref):
    h = h_ref[...]
    hn = (h * _rms_scale(h) * fgain_ref[...]).astype(BF16)
    acc = h
    for c in range(len(FFN_FF_SPLITS) - 1):
        sl = slice(FFN_FF_SPLITS[c], FFN_FF_SPLITS[c + 1])
        gate = _dot(hn, wg_ref[:, sl])
        up = _dot(hn, wu_ref[:, sl])
        hid = (gate * _sigmoid(gate) * up).astype(BF16)
        acc = acc + _dot(hid, wd_ref[sl, :])
    h2_ref[...] = acc
    normed = acc * _rms_scale(acc)
    kv = _dot((normed * kvgain_ref[...]).astype(BF16), kvw_ref[...])
    nk = SWA_KV_HEADS * SWA_KV_LANES
    for g in range(SWA_KV_HEADS):
        sl = slice(g * SWA_KV_LANES, (g + 1) * SWA_KV_LANES)
        kg = kv[:, sl]
        k_ref[:, sl] = (kg * _rms_scale(kg) * kgain_ref[...]).astype(BF16)
    v_ref[...] = kv[:, nk:].astype(BF16)
    q_ref[...] = _dot((normed * qgain_ref[...]).astype(BF16), wq_ref[...]).astype(BF16)


def _dup_kv_columns(w):
    d = w.shape[0]
    w = w.reshape(d, SWA_KV_HEADS, 1, SWA_HEAD_DIM)
    return jnp.broadcast_to(w, (d, SWA_KV_HEADS, 2, SWA_HEAD_DIM)).reshape(d, SWA_KV_HEADS * SWA_KV_LANES)


def _ffn_kvq(h1, fgain, wg, wu, wd, kvgain, kv_w, k_gain, qgain, wq):
    t, d = h1.shape
    rows = min(FFN_ROWS, t)
    nkv = SWA_KV_HEADS * SWA_HEAD_DIM
    kvw2 = jnp.concatenate([_dup_kv_columns(kv_w[:, :nkv]), _dup_kv_columns(kv_w[:, nkv:])],
                           axis=1).astype(BF16)
    kgain2 = jnp.concatenate([k_gain, k_gain]).reshape(1, SWA_KV_LANES)
    nk = SWA_KV_HEADS * SWA_KV_LANES
    row_spec = lambda w: pl.BlockSpec((rows, w), lambda i: (i, 0))
    return pl.pallas_call(
        _ffn_kvq_kernel,
        out_shape=(jax.ShapeDtypeStruct((t, d), F32), jax.ShapeDtypeStruct((t, d), BF16),
                   jax.ShapeDtypeStruct((t, nk), BF16), jax.ShapeDtypeStruct((t, nk), BF16)),
        grid=(t // rows,),
        in_specs=[
            row_spec(d),
            _const_spec((1, d)),
            _const_spec((d, D_FF_DENSE)), _const_spec((d, D_FF_DENSE)), _const_spec((D_FF_DENSE, d)),
            _const_spec((1, d)), _const_spec((d, 2 * nk)), _const_spec((1, SWA_KV_LANES)),
            _const_spec((1, d)), _const_spec((d, d)),
        ],
        out_specs=(row_spec(d), row_spec(d), row_spec(nk), row_spec(nk)),
        compiler_params=pltpu.CompilerParams(
            dimension_semantics=("arbitrary",), vmem_limit_bytes=VMEM_LIMIT_BYTES),
        name="ffn_kvq",
    )(h1, fgain.reshape(1, d), wg, wu, wd, kvgain.reshape(1, d), kvw2, kgain2,
      qgain.reshape(1, d), wq)


ATTN_ROWS = 256
REL_BUCKETS = 32
REL_MAX_DIST = 128
N_EXPERTS = 8
ROUTE_LANES = 128
ROUTE_COLS = 8


def _rel_bucket_table():
    w = SWA_WINDOW
    qi = np.arange(w)[:, None]
    kj = np.arange(2 * w)[None, :]
    dist = qi + w - kj
    in_win = (dist >= 0) & (dist < w)
    n = np.maximum(dist, 0)
    max_exact = REL_BUCKETS // 2
    nf = np.maximum(n, 1).astype(np.float64)
    large = max_exact + (np.log(nf / max_exact) / np.log(REL_MAX_DIST / max_exact)
                         * (REL_BUCKETS - max_exact)).astype(np.int64)
    large = np.minimum(large, REL_BUCKETS - 1)
    bucket = np.where(n < max_exact, n, large)
    return np.where(in_win, bucket, -1).astype(np.int32)


def _attn_router_kernel(tiles_per_seq, relb_ref, sink_ref, q_ref, kc_ref, kp_ref, vc_ref, vp_ref,
                        h2_ref, bucket_ref, qgain_ref, wo_ref, fgain_ref, wr_ref,
                        h3_ref, hn2_ref, route_ref, bias_ref, attn_ref, hnprev_ref):
    w = SWA_WINDOW
    hd = SWA_HEAD_DIM
    rows = q_ref.shape[1]
    step = pl.program_id(0)
    last_tile = pl.num_programs(0) - 2
    first_tile = jnp.minimum(step, last_tile) % tiles_per_seq == 0

    @pl.when(step == 0)
    def _():
        hnprev_ref[...] = jnp.zeros_like(hnprev_ref)
        bucket = bucket_ref[...]
        for h in range(SWA_Q_HEADS):
            plane = jnp.full((w, 2 * w), -jnp.inf, F32)
            for bkt in range(REL_BUCKETS):
                plane = jnp.where(bucket == bkt, relb_ref[bkt * SWA_Q_HEADS + h], plane)
            bias_ref[h] = plane

    hp = hnprev_ref[...]
    hi = hp.astype(BF16)
    lo = (hp - hi.astype(F32)).astype(BF16)
    logits = _dot(hi, wr_ref[0]) + _dot(lo, wr_ref[0]) + _dot(hi, wr_ref[1])
    col = lax.broadcasted_iota(jnp.int32, logits.shape, 1)
    logits = jnp.where(col < N_EXPERTS, logits, -jnp.inf)
    m0 = jnp.max(logits, axis=-1, keepdims=True)
    i0 = jnp.min(jnp.where(logits == m0, col, ROUTE_LANES), axis=-1, keepdims=True)
    rest = jnp.where(col == i0, -jnp.inf, logits)
    m1 = jnp.max(rest, axis=-1, keepdims=True)
    i1 = jnp.min(jnp.where(rest == m1, col, ROUTE_LANES), axis=-1, keepdims=True)
    e1 = jnp.exp(m1 - m0)
    g0 = 1.0 / (1.0 + e1)
    g1 = e1 / (1.0 + e1)
    rc = lax.broadcasted_iota(jnp.int32, (rows, ROUTE_COLS), 1)
    route_ref[0] = jnp.where(rc == 0, i0.astype(F32),
                             jnp.where(rc == 1, i1.astype(F32),
                                       jnp.where(rc == 2, g0, jnp.where(rc == 3, g1, 0.0))))


    lane = lax.broadcasted_iota(jnp.int32, (2 * w, SWA_KV_LANES), 1)
    left = lane < hd
    lane_q = lax.broadcasted_iota(jnp.int32, (w, SWA_KV_LANES), 1)
    left_q = lane_q < hd
    key_idx = lax.broadcasted_iota(jnp.int32, (1, 2 * w), 1)
    qgain = qgain_ref[...]

    k_all = jnp.concatenate([kp_ref[0], kc_ref[0]], axis=0)
    v_all = jnp.concatenate([vp_ref[0], vc_ref[0]], axis=0)
    zero = jnp.zeros((), BF16)
    no_prev = jnp.where(jnp.logical_and(first_tile, key_idx < w), -jnp.inf, 0.0)
    units = [(blk, g) for blk in range(rows // w) for g in range(SWA_KV_HEADS)]

    score, values = [], []
    for blk, g in units:
        r0 = blk * w
        ks = slice(g * SWA_KV_LANES, (g + 1) * SWA_KV_LANES)
        k2 = k_all[r0:r0 + 2 * w, ks]
        v2 = v_all[r0:r0 + 2 * w, ks]
        k_bd = jnp.concatenate([jnp.where(left, k2, zero), jnp.where(left, zero, k2)], axis=0)
        values.append(jnp.concatenate([jnp.where(left, v2, zero), jnp.where(left, zero, v2)], axis=0))
        pairs = []
        for p in range(2):
            c0 = (4 * g + 2 * p) * hd
            qp = q_ref[0, r0:r0 + w, c0:c0 + 2 * hd].astype(F32)
            sq = qp * qp
            s_l = jnp.sum(jnp.where(left_q, sq, 0.0), axis=-1, keepdims=True)
            s_r = jnp.sum(jnp.where(left_q, 0.0, sq), axis=-1, keepdims=True)
            ms = jnp.where(left_q, s_l, s_r) * (1.0 / hd)
            pairs.append((qp * lax.rsqrt(ms + NORM_EPS) * qgain * (hd ** -0.5)).astype(BF16))
        qs = jnp.concatenate(pairs, axis=0)
        score.append(_dot_nt(qs, k_bd))

    probs, sinks = [], []
    for (blk, g), s in zip(units, score):
        e_rows, sink_rows = [], []
        for p in range(2):
            halves = []
            for side in range(2):
                hq = 4 * g + 2 * p + side
                logit = s[p * w:(p + 1) * w, side * 2 * w:(side + 1) * 2 * w] + bias_ref[hq]
                if blk == 0:
                    logit = logit + no_prev
                sink = sink_ref[hq]
                m = jnp.maximum(jnp.max(logit, axis=-1, keepdims=True), sink)
                halves.append((jnp.exp(logit - m), jnp.exp(sink - m)))
            e_rows.append(jnp.concatenate([halves[0][0], halves[1][0]], axis=1).astype(BF16))
            sink_rows.append(jnp.where(left_q, halves[0][1], halves[1][1]))
        probs.append(jnp.concatenate(e_rows, axis=0))
        sinks.append(jnp.concatenate(sink_rows, axis=0))

    ones_bd = jnp.concatenate([jnp.where(left, 1.0, 0.0), jnp.where(left, 0.0, 1.0)], axis=0).astype(BF16)
    for (blk, g), e2, v_bd, sink_term in zip(units, probs, values, sinks):
        pvd = _dot(e2, jnp.concatenate([v_bd, ones_bd], axis=1))
        pv = pvd[:, :SWA_KV_LANES] / (pvd[:, SWA_KV_LANES:] + sink_term)
        r0 = blk * w
        for p in range(2):
            c0 = (4 * g + 2 * p) * hd
            attn_ref[r0:r0 + w, c0:c0 + 2 * hd] = pv[p * w:(p + 1) * w].astype(BF16)

    h3 = h2_ref[0] + _dot(attn_ref[...], wo_ref[...])
    h3_ref[0] = h3
    hn2 = h3 * _rms_scale(h3) * fgain_ref[...]
    _store_token_rows(hn2_ref, hn2)
    hnprev_ref[...] = hn2


def _attn_router(q, k2, v2, h2, rel_bias, sink, q_gain, w_o, fgain, w_router):
    bsz, seq, d = h2.shape
    rows = min(ATTN_ROWS, seq)
    w = SWA_WINDOW
    nk = SWA_KV_HEADS * SWA_KV_LANES
    per = rows // w
    nj = seq // rows
    n_tiles = bsz * nj
    bucket = jnp.asarray(_rel_bucket_table())
    qgain2 = jnp.concatenate([q_gain, q_gain]).reshape(1, SWA_KV_LANES)
    wr = jnp.zeros((d, ROUTE_LANES), F32).at[:, :N_EXPERTS].set(w_router)
    wr_hi = wr.astype(BF16)
    wr_lo = (wr - wr_hi.astype(F32)).astype(BF16)
    wr2 = jnp.stack([wr_hi, wr_lo])
    cur = lambda s: jnp.minimum(s, n_tiles - 1)
    tile = lambda width: pl.BlockSpec((1, rows, width), lambda s, *_: (cur(s) // nj, cur(s) % nj, 0))
    prev = pl.BlockSpec(
        (1, w, nk), lambda s, *_: (cur(s) // nj, jnp.maximum((cur(s) % nj) * per - 1, 0), 0))
    routed = lambda s: jnp.maximum(s - 1, 0)
    const = lambda shape: pl.BlockSpec(shape, lambda s, *_: (0,) * len(shape),
                                       pipeline_mode=pl.Buffered(1))
    return pl.pallas_call(
        functools.partial(_attn_router_kernel, nj),
        out_shape=(jax.ShapeDtypeStruct((bsz, seq, d), F32),
                   jax.ShapeDtypeStruct((bsz * seq * ROW_CHUNKS, 128), F32),
                   jax.ShapeDtypeStruct((bsz, seq, ROUTE_COLS), F32)),
        grid_spec=pltpu.PrefetchScalarGridSpec(
            num_scalar_prefetch=2,
            grid=(n_tiles + 1,),
            in_specs=[tile(d), tile(nk), prev, tile(nk), prev, tile(d),
                      const((w, 2 * w)), const((1, SWA_KV_LANES)), const((d, d)), const((1, d)),
                      const((2, d, ROUTE_LANES))],
            out_specs=(tile(d),
                       pl.BlockSpec((rows * ROW_CHUNKS, 128), lambda s, *_: (cur(s), 0)),
                       pl.BlockSpec((1, rows, ROUTE_COLS),
                                    lambda s, *_: (routed(s) // nj, routed(s) % nj, 0))),
            scratch_shapes=[pltpu.VMEM((SWA_Q_HEADS, w, 2 * w), F32), pltpu.VMEM((rows, d), BF16),
                            pltpu.VMEM((rows, d), F32)],
        ),
        compiler_params=pltpu.CompilerParams(
            dimension_semantics=("arbitrary",), vmem_limit_bytes=VMEM_LIMIT_BYTES),
        name="attn_router",
    )(rel_bias.reshape(-1), sink, q, k2, k2, v2, v2, h2, bucket, qgain2, w_o, fgain.reshape(1, d), wr2)


TOP_K = 2
D_FF_EXPERT = 3584
MOE_TILE = 512
MOE_FF_SUB = 512
RANK_BLOCK = 1024
DISPATCH_BLOCK = 2048
COMBINE_BLOCK = 512
ZERO_ROWS = 256
RANK_SUBLANES = 16
DMA_PRIORITIES = 2


def _rank_kernel(e_ref, tri_ref, rank_ref, cnt_ref, carry_ref):
    bt = e_ref.shape[1]

    @pl.when(pl.program_id(0) == 0)
    def _():
        carry_ref[...] = jnp.zeros_like(carry_ref)

    ex = lax.broadcasted_iota(jnp.int32, (RANK_SUBLANES, bt), 0)
    oh0 = ex == e_ref[0:1, :]
    oh1 = ex == e_ref[1:2, :]
    both = oh0.astype(F32) + oh1.astype(F32)
    pos = _dot(both.astype(BF16), tri_ref[...]) + carry_ref[:, 0:1]
    rank_ref[0:1, :] = jnp.sum(jnp.where(oh0, pos, 0.0), axis=0, keepdims=True).astype(jnp.int32)
    rank_ref[1:2, :] = jnp.sum(jnp.where(oh1, pos, 0.0), axis=0, keepdims=True).astype(jnp.int32)
    carry_ref[...] = carry_ref[...] + jnp.sum(both, axis=1, keepdims=True)
    cnt_ref[...] = carry_ref[...]


def _route_ranks(experts):
    _, t = experts.shape
    bt = min(RANK_BLOCK, t)
    tri = jnp.asarray(np.triu(np.ones((bt, bt), np.float32), k=1), dtype=BF16)
    rank, cnt = pl.pallas_call(
        _rank_kernel,
        out_shape=(jax.ShapeDtypeStruct((TOP_K, t), jnp.int32),
                   jax.ShapeDtypeStruct((RANK_SUBLANES, 128), F32)),
        grid=(t // bt,),
        in_specs=[pl.BlockSpec((TOP_K, bt), lambda i: (0, i)), _const_spec((bt, bt))],
        out_specs=(pl.BlockSpec((TOP_K, bt), lambda i: (0, i)),
                   pl.BlockSpec((RANK_SUBLANES, 128), lambda i: (0, 0))),
        scratch_shapes=[pltpu.VMEM((RANK_SUBLANES, 128), F32)],
        compiler_params=pltpu.CompilerParams(dimension_semantics=("arbitrary",)),
        name="route_rank",
    )(experts, tri)
    return rank, cnt[:N_EXPERTS, 0].astype(jnp.int32)


def _row_copy(src, src_token, dst, dst_token, sem):
    return pltpu.make_async_copy(_token_row_slice(src, src_token), _token_row_slice(dst, dst_token), sem)


def _dispatch_kernel(start_ref, zrow_ref, nz_ref, hn_ref, e_ref, rank_ref, xs_ref, dest_ref,
                     dvm_ref, dsm_ref, zero_ref, sem_idx, sem_rows, sem_zero):
    bt = e_ref.shape[1]
    zero_copies = MOE_TILE // ZERO_ROWS

    def zero_copy(z, c):
        row = pl.multiple_of((zrow_ref[z] + c * ZERO_ROWS) * ROW_CHUNKS, ROW_CHUNKS)
        return pltpu.make_async_copy(zero_ref, xs_ref.at[pl.ds(row, ZERO_ROWS * ROW_CHUNKS)], sem_zero)

    @pl.when(pl.program_id(0) == 0)
    def _():
        zero_ref[...] = jnp.zeros_like(zero_ref)

        def start(z, carry):
            for c in range(zero_copies):
                zero_copy(z, c).start()
            return carry

        def wait(z, carry):
            for c in range(zero_copies):
                zero_copy(z, c).wait()
            return carry

        lax.fori_loop(0, nz_ref[0], start, 0)
        lax.fori_loop(0, nz_ref[0], wait, 0)

    e = e_ref[...]
    base = jnp.zeros_like(e)
    for x in range(N_EXPERTS):
        base = jnp.where(e == x, start_ref[x], base)
    dest = base + rank_ref[...]
    dest_ref[...] = dest
    dvm_ref[...] = dest
    idx_copy = pltpu.make_async_copy(dvm_ref, dsm_ref, sem_idx)
    idx_copy.start()
    idx_copy.wait()

    def issue(t, carry):
        for c in range(TOP_K):
            _row_copy(hn_ref, t, xs_ref, dsm_ref[c, t], sem_rows).start(priority=c % DMA_PRIORITIES)
        return carry

    lax.fori_loop(0, bt, issue, 0, unroll=8)
    for c in range(TOP_K):
        pltpu.make_async_copy(hn_ref, xs_ref.at[pl.ds(0, bt * ROW_CHUNKS)], sem_rows).wait()


def _dispatch(hn, experts, rank, start, zrows, nz, m_pad):
    t = hn.shape[0] // ROW_CHUNKS
    bt = min(DISPATCH_BLOCK, t)
    blk = lambda i, *_: (0, i)
    return pl.pallas_call(
        _dispatch_kernel,
        out_shape=(jax.ShapeDtypeStruct((m_pad * ROW_CHUNKS, 128), F32),
                   jax.ShapeDtypeStruct((TOP_K, t), jnp.int32)),
        grid_spec=pltpu.PrefetchScalarGridSpec(
            num_scalar_prefetch=3,
            grid=(t // bt,),
            in_specs=[pl.BlockSpec((bt * ROW_CHUNKS, 128), lambda i, *_: (i, 0)),
                      pl.BlockSpec((TOP_K, bt), blk), pl.BlockSpec((TOP_K, bt), blk)],
            out_specs=(pl.BlockSpec(memory_space=pl.ANY), pl.BlockSpec((TOP_K, bt), blk)),
            scratch_shapes=[pltpu.VMEM((TOP_K, bt), jnp.int32), pltpu.SMEM((TOP_K, bt), jnp.int32),
                            pltpu.VMEM((ZERO_ROWS * ROW_CHUNKS, 128), F32),
                            pltpu.SemaphoreType.DMA, pltpu.SemaphoreType.DMA, pltpu.SemaphoreType.DMA],
        ),
        compiler_params=pltpu.CompilerParams(dimension_semantics=("arbitrary",)),
        name="moe_dispatch",
    )(start, zrows, nz, hn, experts, rank)


def _expert_kernel(te_ref, tv_ref, x_ref, wg_ref, wu_ref, wd_ref, out_ref):
    valid = tv_ref[pl.program_id(0)] == 1

    @pl.when(valid)
    def _():
        xb = _load_token_rows(x_ref, MOE_TILE).astype(BF16)
        acc = jnp.zeros((MOE_TILE, D_MODEL), F32)
        for j in range(D_FF_EXPERT // MOE_FF_SUB):
            sl = slice(j * MOE_FF_SUB, (j + 1) * MOE_FF_SUB)
            gate = _dot(xb, wg_ref[0, :, sl])
            up = _dot(xb, wu_ref[0, :, sl])
            hid = (gate * _sigmoid(gate) * up).astype(BF16)
            acc = acc + _dot(hid, wd_ref[0, sl, :])
        _store_token_rows(out_ref, acc)

    @pl.when(jnp.logical_not(valid))
    def _():
        out_ref[...] = jnp.zeros_like(out_ref)


def _experts(xs, tile_expert, tile_valid, wg, wu, wd):
    d = D_MODEL
    row_block = pl.BlockSpec((MOE_TILE * ROW_CHUNKS, 128), lambda i, te, tv: (i, 0))
    weights = lambda shape: pl.BlockSpec((1,) + shape, lambda i, te, tv: (te[i], 0, 0),
                                         pipeline_mode=pl.Buffered(1))
    return pl.pallas_call(
        _expert_kernel,
        out_shape=jax.ShapeDtypeStruct(xs.shape, F32),
        grid_spec=pltpu.PrefetchScalarGridSpec(
            num_scalar_prefetch=2,
            grid=(xs.shape[0] // (MOE_TILE * ROW_CHUNKS),),
            in_specs=[row_block, weights((d, D_FF_EXPERT)), weights((d, D_FF_EXPERT)),
                      weights((D_FF_EXPERT, d))],
            out_specs=row_block,
        ),
        compiler_params=pltpu.CompilerParams(
            dimension_semantics=("arbitrary",), vmem_limit_bytes=VMEM_LIMIT_BYTES),
        name="moe_experts",
    )(tile_expert, tile_valid, xs, wg, wu, wd)


def _combine_kernel(h_ref, route_ref, dest_ref, dest_next_ref, yb_ref, out_ref,
                    dsm_ref, ybuf_ref, sem_idx, sem_rows):
    bt = h_ref.shape[0]
    i = pl.program_id(0)
    slot = i % 2

    def start_gathers(step_dest_ref, s):
        idx_copy = pltpu.make_async_copy(step_dest_ref, dsm_ref.at[s], sem_idx)
        idx_copy.start()
        idx_copy.wait()

        def issue(t, carry):
            for c in range(TOP_K):
                _row_copy(yb_ref, dsm_ref[s, c, t], ybuf_ref.at[s, c], t, sem_rows.at[s]).start(
                    priority=c % DMA_PRIORITIES)
            return carry

        lax.fori_loop(0, bt, issue, 0, unroll=8)

    @pl.when(i == 0)
    def _():
        start_gathers(dest_ref, 0)

    @pl.when(i + 1 < pl.num_programs(0))
    def _():
        start_gathers(dest_next_ref, 1 - slot)

    for c in range(TOP_K):
        pltpu.make_async_copy(yb_ref.at[pl.ds(0, bt * ROW_CHUNKS)], ybuf_ref.at[slot, c],
                              sem_rows.at[slot]).wait()
    route = route_ref[...]
    y0 = _load_token_rows(ybuf_ref, bt, (slot, 0))
    y1 = _load_token_rows(ybuf_ref, bt, (slot, 1))
    out_ref[...] = h_ref[...] + route[:, 2:3] * y0 + route[:, 3:4] * y1


def _combine(h3, route, dest, yb):
    t, d = h3.shape
    bt = min(COMBINE_BLOCK, t)
    n = t // bt
    return pl.pallas_call(
        _combine_kernel,
        out_shape=jax.ShapeDtypeStruct((t, d), F32),
        grid=(n,),
        in_specs=[pl.BlockSpec((bt, d), lambda i: (i, 0)),
                  pl.BlockSpec((bt, ROUTE_COLS), lambda i: (i, 0)),
                  pl.BlockSpec((TOP_K, bt), lambda i: (0, i)),
                  pl.BlockSpec((TOP_K, bt), lambda i: (0, jnp.minimum(i + 1, n - 1))),
                  pl.BlockSpec(memory_space=pl.ANY)],
        out_specs=pl.BlockSpec((bt, d), lambda i: (i, 0)),
        scratch_shapes=[pltpu.SMEM((2, TOP_K, bt), jnp.int32),
                        pltpu.VMEM((2, TOP_K, bt * ROW_CHUNKS, 128), F32),
                        pltpu.SemaphoreType.DMA, pltpu.SemaphoreType.DMA((2,))],
        compiler_params=pltpu.CompilerParams(
            dimension_semantics=("arbitrary",), vmem_limit_bytes=VMEM_LIMIT_BYTES),
        name="moe_combine",
    )(h3, route, dest, dest, yb)


def _moe_layer(h3, hn2, route, wg, wu, wd):
    t, d = h3.shape
    n_tiles = (t * TOP_K) // MOE_TILE + N_EXPERTS
    m_pad = n_tiles * MOE_TILE
    experts = route[:, :TOP_K].astype(jnp.int32).T
    rank, counts = _route_ranks(experts)
    tiles_per = (counts + MOE_TILE - 1) // MOE_TILE
    tile_end = jnp.cumsum(tiles_per)
    start = ((tile_end - tiles_per) * MOE_TILE).astype(jnp.int32)
    tile_ids = jnp.arange(n_tiles, dtype=jnp.int32)
    tile_valid = (tile_ids < tile_end[-1]).astype(jnp.int32)
    last_valid = jnp.maximum(tile_end[-1] - 1, 0)
    tile_expert = jnp.minimum(
        jnp.sum(jnp.minimum(tile_ids, last_valid)[:, None] >= tile_end[None, :], axis=1),
        N_EXPERTS - 1).astype(jnp.int32)
    has_pad = (counts % MOE_TILE) != 0
    pad_tile = jnp.where(has_pad, tile_end - 1, n_tiles)
    zmask = jnp.zeros((n_tiles + 1,), jnp.int32).at[pad_tile].set(1)[:n_tiles]
    zmask = jnp.maximum(zmask, 1 - tile_valid)
    nz = jnp.sum(zmask).astype(jnp.int32).reshape(1)
    zorder = jnp.argsort(1 - zmask, stable=True).astype(jnp.int32)
    zrows = (zorder[:2 * N_EXPERTS] * MOE_TILE).astype(jnp.int32)
    xs, dest = _dispatch(hn2, experts, rank, start, zrows, nz, m_pad)
    yb = _experts(xs, tile_expert, tile_valid, wg, wu, wd)
    return _combine(h3, route, dest, yb)


def _stages(x, hgrn_w_in, hgrn_lb, hgrn_gnorm, hgrn_w_out, swa_w_q, swa_q_gain, swa_sink, swa_w_o, kv_norm, kv_w, k_gain, rel_bias, attn_norm, ffn_norm, ffn_w_gate, ffn_w_up, ffn_w_down, moe_router, moe_w_gate, moe_w_up, moe_w_down):
    bsz, seq, d = x.shape
    t = bsz * seq
    bf = lambda w: w.astype(BF16)
    h1 = _hgrn_layer(x, attn_norm[0], bf(hgrn_w_in[0]), hgrn_lb, hgrn_gnorm[0], bf(hgrn_w_out[0]))
    h2, q, k2, v2 = _ffn_kvq(h1.reshape(t, d), ffn_norm[0], bf(ffn_w_gate[0]), bf(ffn_w_up[0]),
                             bf(ffn_w_down[0]), kv_norm, kv_w, k_gain, attn_norm[1], bf(swa_w_q[0]))
    nk = SWA_KV_HEADS * SWA_KV_LANES
    h3, hn2, route = _attn_router(q.reshape(bsz, seq, d), k2.reshape(bsz, seq, nk),
                                  v2.reshape(bsz, seq, nk), h2.reshape(bsz, seq, d), rel_bias,
                                  swa_sink[0], swa_q_gain[0], bf(swa_w_o[0]), ffn_norm[1],
                                  moe_router[0])
    h4 = _moe_layer(h3.reshape(t, d), hn2, route.reshape(t, ROUTE_COLS),
                    bf(moe_w_gate[0]), bf(moe_w_up[0]), bf(moe_w_down[0]))
    return {"h1": h1, "h2": h2.reshape(bsz, seq, d), "h3": h3, "h4": h4.reshape(bsz, seq, d)}


def kernel(x, hgrn_w_in, hgrn_lb, hgrn_gnorm, hgrn_w_out, swa_w_q, swa_q_gain, swa_sink, swa_w_o, kv_norm, kv_w, k_gain, rel_bias, attn_norm, ffn_norm, ffn_w_gate, ffn_w_up, ffn_w_down, moe_router, moe_w_gate, moe_w_up, moe_w_down):
    return _stages(x, hgrn_w_in, hgrn_lb, hgrn_gnorm, hgrn_w_out, swa_w_q, swa_q_gain, swa_sink, swa_w_o, kv_norm, kv_w, k_gain, rel_bias, attn_norm, ffn_norm, ffn_w_gate, ffn_w_up, ffn_w_down, moe_router, moe_w_gate, moe_w_up, moe_w_down)["h4"]
```

```python
import functools

import numpy as np
import jax
import jax.numpy as jnp
from jax import lax
from jax.experimental import pallas as pl
from jax.experimental.pallas import tpu as pltpu

F32 = jnp.float32
BF16 = jnp.bfloat16

D_MODEL = 1024
NORM_EPS = 1e-6
LOG2_E = 1.4426950408889634

LANES = 128
MXU_WIDTH = 256

HG_HEADS = 8
HG_DK = 128
HG_DV = 128
HG_CHUNK = 128
HG_LEVELS = 7
HG_STEP_CHUNKS = 4
HG_GROUP_HEADS = 8

VMEM_LIMIT_BYTES = 56 * 1024 * 1024


def _dot(a, b):
    return jnp.dot(a, b, preferred_element_type=F32)


def _dot_nt(a, b):
    return lax.dot_general(a, b, (((1,), (1,)), ((), ())), preferred_element_type=F32)


def _dot_tn(a, b):
    return lax.dot_general(a, b, (((0,), (0,)), ((), ())), preferred_element_type=F32)


def _rms_scale(x):
    return lax.rsqrt(jnp.mean(x * x, axis=-1, keepdims=True) + NORM_EPS)


def _sigmoid(x):
    return 1.0 / (1.0 + jnp.exp(-x))


ROW_CHUNKS = D_MODEL // LANES


def _load_token_rows(ref, n, lead=()):
    return jnp.concatenate(
        [ref[lead + (pl.ds(c, n, stride=ROW_CHUNKS), slice(None))] for c in range(ROW_CHUNKS)], axis=1)


def _store_token_rows(ref, x):
    n = x.shape[0]
    for c in range(ROW_CHUNKS):
        ref[pl.ds(c, n, stride=ROW_CHUNKS), :] = x[:, c * LANES:(c + 1) * LANES]


def _token_row_slice(ref, token):
    return ref.at[pl.ds(pl.multiple_of(token * ROW_CHUNKS, ROW_CHUNKS), ROW_CHUNKS)]


def _const_spec(shape):
    nd = len(shape)
    return pl.BlockSpec(shape, lambda *_: (0,) * nd, pipeline_mode=pl.Buffered(1))


def _hgrn_level_log_decay(b, level):
    c = HG_CHUNK
    m = 1 << level
    parts = []
    if 2 * m >= 8:
        for start in range(0, c, 2 * m):
            mid = start + m - 1
            parts.append(b[start:start + 2 * m] - b[mid:mid + 1])
    else:
        first_block = lax.broadcasted_iota(jnp.int32, (8, b.shape[1]), 0) < 4
        for start in range(0, c, 8):
            b_mid = jnp.where(first_block, b[start + 1:start + 2], b[start + 5:start + 6])
            parts.append(b[start:start + 8] - b_mid)
    return jnp.concatenate(parts, axis=0)


def _hgrn_kernel(x_ref, gain_ref, win_ref, lbp_ref, tsum_ref, gnorm_ref, wout_ref,
                 out_ref, st_ref, o_ref):
    c = HG_CHUNK

    @pl.when(pl.program_id(1) == 0)
    def _():
        st_ref[...] = jnp.zeros_like(st_ref)

    chunks = [slice(ci * c, (ci + 1) * c) for ci in range(HG_STEP_CHUNKS)]

    lbp = lbp_ref[...]
    lbe = jnp.exp(lbp - jnp.max(lbp, axis=0, keepdims=True))
    lb = lbe[0:1] / jnp.sum(lbe, axis=0, keepdims=True)

    x = x_ref[0]
    hn = (x * _rms_scale(x) * gain_ref[...]).astype(BF16)

    ti = lax.broadcasted_iota(jnp.int32, (c, c), 0)
    si = lax.broadcasted_iota(jnp.int32, (c, c), 1)
    diag = ti == si
    xor = ti ^ si
    lv_mask = [((xor >> l) == 1) & (((ti >> l) & 1) == 1) for l in range(HG_LEVELS)]
    keep = [jnp.where(m, 1.0, 0.0) for m in [diag] + lv_mask]

    row = lax.broadcasted_iota(jnp.int32, (c, HG_DK), 0)
    upper_rows = [((row >> l) & 1) == 1 for l in range(HG_LEVELS)]
    signs = [jnp.where(u, LOG2_E, -LOG2_E) for u in upper_rows]

    gw = HG_GROUP_HEADS * HG_DK
    for p in range(HG_HEADS // HG_GROUP_HEADS):
        cs = slice(p * gw, (p + 1) * gw)
        sec = [_dot(hn, win_ref[:, i * D_MODEL + p * gw:i * D_MODEL + (p + 1) * gw]) for i in range(4)]
        f = lb[:, cs] + (1.0 - lb[:, cs]) * _sigmoid(sec[1])
        g = jnp.log(f)
        g_hi = g.astype(BF16)
        g_lo = (g - g_hi.astype(F32)).astype(BF16)
        b = [_dot(tsum_ref[...], jnp.concatenate([g_hi[rs], g_lo[rs]], axis=0)) for rs in chunks]

        staged = []
        for ci, rs in enumerate(chunks):
            for hh in range(HG_GROUP_HEADS):
                ls = slice(hh * HG_DK, (hh + 1) * HG_DK)
                q_h = sec[0][rs, ls]
                f_h = f[rs, ls]
                b_h = b[ci][:, ls]
                k_h = 1.0 - f_h
                b_last = b_h[c - 1:c]
                per_level = []
                for l in range(HG_LEVELS):
                    if l == 0:
                        w = jnp.where(upper_rows[0], f_h, 1.0)
                    else:
                        w = jnp.exp2(_hgrn_level_log_decay(b_h, l) * signs[l])
                    per_level.append((jnp.where(upper_rows[l], q_h, k_h) * w).astype(BF16))
                staged.append((rs, p * HG_GROUP_HEADS + hh, dict(
                    levels=per_level, q=q_h.astype(BF16), k=k_h.astype(BF16),
                    qb=(q_h * jnp.exp(b_h)).astype(BF16),
                    khat=(k_h * jnp.exp(b_last - b_h)).astype(BF16),
                    decay=jnp.exp(b_last),
                    v=sec[2][rs, ls].astype(BF16),
                    o_gate=sec[3][rs, ls])))

        scores = []
        for _, _, s in staged:
            a = _dot_nt(s["q"], s["k"]) * keep[0]
            for m, lv in zip(keep[1:], s["levels"]):
                a = a + _dot_nt(lv, lv) * m
            scores.append(a.astype(BF16))

        outs = []
        for (_, h, s), a in zip(staged, scores):
            st = st_ref[h]
            outs.append(_dot(a, s["v"]) + _dot_nt(s["qb"], st.astype(BF16)))
            st_ref[h] = st * s["decay"] + _dot_tn(s["v"], s["khat"])

        for (rs, h, s), o_h in zip(staged, outs):
            o_h = o_h * _rms_scale(o_h) * gnorm_ref[...]
            og = s["o_gate"]
            o_ref[rs, h * HG_DV:(h + 1) * HG_DV] = (o_h * (og * _sigmoid(og))).astype(BF16)

    out_ref[0] = x + _dot(o_ref[...], wout_ref[...])


def _hgrn_layer(x, gain, w_in, lb_param, gnorm, w_out):
    bsz, seq, d = x.shape
    c = HG_CHUNK * HG_STEP_CHUNKS
    tril = np.tril(np.ones((HG_CHUNK, HG_CHUNK), np.float32))
    tsum2 = jnp.asarray(np.concatenate([tril, tril], axis=1), dtype=BF16)
    return pl.pallas_call(
        _hgrn_kernel,
        out_shape=jax.ShapeDtypeStruct((bsz, seq, d), F32),
        grid=(bsz, seq // c),
        in_specs=[
            pl.BlockSpec((1, c, d), lambda b, j: (b, j, 0)),
            _const_spec((1, d)),
            _const_spec((d, 4 * d)),
            _const_spec(lb_param.shape),
            _const_spec(tsum2.shape),
            _const_spec((1, HG_DV)),
            _const_spec((d, d)),
        ],
        out_specs=pl.BlockSpec((1, c, d), lambda b, j: (b, j, 0)),
        scratch_shapes=[
            pltpu.VMEM((HG_HEADS, HG_DV, HG_DK), F32),
            pltpu.VMEM((c, d), BF16),
        ],
        compiler_params=pltpu.CompilerParams(
            dimension_semantics=("arbitrary", "arbitrary"),
            vmem_limit_bytes=VMEM_LIMIT_BYTES),
        name="hgrn_layer",
    )(x, gain.reshape(1, d), w_in, lb_param, tsum2, gnorm.reshape(1, HG_DV), w_out)


D_FF_DENSE = 2816
FFN_FF_SPLITS = tuple(MXU_WIDTH * n for n in (0, 3, 6, 9, 11))
FFN_ROWS = 512

SWA_HEAD_DIM = 64
SWA_Q_HEADS = 16
SWA_KV_HEADS = 4
SWA_WINDOW = 128
SWA_KV_LANES = 2 * SWA_HEAD_DIM


def _ffn_kvq_kernel(h_ref, fgain_ref, wg_ref, wu_ref, wd_ref, kvgain_ref, kvw_ref, kgain_ref,
                    qgain_ref, wq_ref, h2_ref, q_ref, k_ref, v_ref):
    h = h_ref[...]
    hn = (h * _rms_scale(h) * fgain_ref[...]).astype(BF16)
    acc = h
    for c in range(len(FFN_FF_SPLITS) - 1):
        sl = slice(FFN_FF_SPLITS[c], FFN_FF_SPLITS[c + 1])
        gate = _dot(hn, wg_ref[:, sl])
        up = _dot(hn, wu_ref[:, sl])
        hid = (gate * _sigmoid(gate) * up).astype(BF16)
        acc = acc + _dot(hid, wd_ref[sl, :])
    h2_ref[...] = acc
    normed = acc * _rms_scale(acc)
    kv = _dot((normed * kvgain_ref[...]).astype(BF16), kvw_ref[...])
    nk = SWA_KV_HEADS * SWA_KV_LANES
    for g in range(SWA_KV_HEADS):
        sl = slice(g * SWA_KV_LANES, (g + 1) * SWA_KV_LANES)
        kg = kv[:, sl]
        k_ref[:, sl] = (kg * _rms_scale(kg) * kgain_ref[...]).astype(BF16)
    v_ref[...] = kv[:, nk:].astype(BF16)
    q_ref[...] = _dot((normed * qgain_ref[...]).astype(BF16), wq_ref[...]).astype(BF16)


def _dup_kv_columns(w):
    d = w.shape[0]
    w = w.reshape(d, SWA_KV_HEADS, 1, SWA_HEAD_DIM)
    return jnp.broadcast_to(w, (d, SWA_KV_HEADS, 2, SWA_HEAD_DIM)).reshape(d, SWA_KV_HEADS * SWA_KV_LANES)


def _ffn_kvq(h1, fgain, wg, wu, wd, kvgain, kv_w, k_gain, qgain, wq):
    t, d = h1.shape
    rows = min(FFN_ROWS, t)
    nkv = SWA_KV_HEADS * SWA_HEAD_DIM
    kvw2 = jnp.concatenate([_dup_kv_columns(kv_w[:, :nkv]), _dup_kv_columns(kv_w[:, nkv:])],
                           axis=1).astype(BF16)
    kgain2 = jnp.concatenate([k_gain, k_gain]).reshape(1, SWA_KV_LANES)
    nk = SWA_KV_HEADS * SWA_KV_LANES
    row_spec = lambda w: pl.BlockSpec((rows, w), lambda i: (i, 0))
    return pl.pallas_call(
        _ffn_kvq_kernel,
        out_shape=(jax.ShapeDtypeStruct((t, d), F32), jax.ShapeDtypeStruct((t, d), BF16),
                   jax.ShapeDtypeStruct((t, nk), BF16), jax.ShapeDtypeStruct((t, nk), BF16)),
        grid=(t // rows,),
        in_specs=[
            row_spec(d),
            _const_spec((1, d)),
            _const_spec((d, D_FF_DENSE)), _const_spec((d, D_FF_DENSE)), _const_spec((D_FF_DENSE, d)),
            _const_spec((1, d)), _const_spec((d, 2 * nk)), _const_spec((1, SWA_KV_LANES)),
            _const_spec((1, d)), _const_spec((d, d)),
        ],
        out_specs=(row_spec(d), row_spec(d), row_spec(nk), row_spec(nk)),
        compiler_params=pltpu.CompilerParams(
            dimension_semantics=("arbitrary",), vmem_limit_bytes=VMEM_LIMIT_BYTES),
        name="ffn_kvq",
    )(h1, fgain.reshape(1, d), wg, wu, wd, kvgain.reshape(1, d), kvw2, kgain2,
      qgain.reshape(1, d), wq)


ATTN_ROWS = 256
REL_BUCKETS = 32
REL_MAX_DIST = 128
N_EXPERTS = 8
ROUTE_LANES = 128
ROUTE_COLS = 8


def _rel_bucket_table():
    w = SWA_WINDOW
    qi = np.arange(w)[:, None]
    kj = np.arange(2 * w)[None, :]
    dist = qi + w - kj
    in_win = (dist >= 0) & (dist < w)
    n = np.maximum(dist, 0)
    max_exact = REL_BUCKETS // 2
    nf = np.maximum(n, 1).astype(np.float64)
    large = max_exact + (np.log(nf / max_exact) / np.log(REL_MAX_DIST / max_exact)
                         * (REL_BUCKETS - max_exact)).astype(np.int64)
    large = np.minimum(large, REL_BUCKETS - 1)
    bucket = np.where(n < max_exact, n, large)
    return np.where(in_win, bucket, -1).astype(np.int32)


def _attn_router_kernel(tiles_per_seq, relb_ref, sink_ref, q_ref, kc_ref, kp_ref, vc_ref, vp_ref,
                        h2_ref, bucket_ref, qgain_ref, wo_ref, fgain_ref, wr_ref,
                        h3_ref, hn2_ref, route_ref, bias_ref, attn_ref, hnprev_ref):
    w = SWA_WINDOW
    hd = SWA_HEAD_DIM
    rows = q_ref.shape[1]
    step = pl.program_id(0)
    last_tile = pl.num_programs(0) - 2
    first_tile = jnp.minimum(step, last_tile) % tiles_per_seq == 0

    @pl.when(step == 0)
    def _():
        hnprev_ref[...] = jnp.zeros_like(hnprev_ref)
        bucket = bucket_ref[...]
        for h in range(SWA_Q_HEADS):
            plane = jnp.full((w, 2 * w), -jnp.inf, F32)
            for bkt in range(REL_BUCKETS):
                plane = jnp.where(bucket == bkt, relb_ref[bkt * SWA_Q_HEADS + h], plane)
            bias_ref[h] = plane

    hp = hnprev_ref[...]
    hi = hp.astype(BF16)
    lo = (hp - hi.astype(F32)).astype(BF16)
    logits = _dot(hi, wr_ref[0]) + _dot(lo, wr_ref[0]) + _dot(hi, wr_ref[1])
    col = lax.broadcasted_iota(jnp.int32, logits.shape, 1)
    logits = jnp.where(col < N_EXPERTS, logits, -jnp.inf)
    m0 = jnp.max(logits, axis=-1, keepdims=True)
    i0 = jnp.min(jnp.where(logits == m0, col, ROUTE_LANES), axis=-1, keepdims=True)
    rest = jnp.where(col == i0, -jnp.inf, logits)
    m1 = jnp.max(rest, axis=-1, keepdims=True)
    i1 = jnp.min(jnp.where(rest == m1, col, ROUTE_LANES), axis=-1, keepdims=True)
    e1 = jnp.exp(m1 - m0)
    g0 = 1.0 / (1.0 + e1)
    g1 = e1 / (1.0 + e1)
    rc = lax.broadcasted_iota(jnp.int32, (rows, ROUTE_COLS), 1)
    route_ref[0] = jnp.where(rc == 0, i0.astype(F32),
                             jnp.where(rc == 1, i1.astype(F32),
                                       jnp.where(rc == 2, g0, jnp.where(rc == 3, g1, 0.0))))


    lane = lax.broadcasted_iota(jnp.int32, (2 * w, SWA_KV_LANES), 1)
    left = lane < hd
    lane_q = lax.broadcasted_iota(jnp.int32, (w, SWA_KV_LANES), 1)
    left_q = lane_q < hd
    key_idx = lax.broadcasted_iota(jnp.int32, (1, 2 * w), 1)
    qgain = qgain_ref[...]

    k_all = jnp.concatenate([kp_ref[0], kc_ref[0]], axis=0)
    v_all = jnp.concatenate([vp_ref[0], vc_ref[0]], axis=0)
    zero = jnp.zeros((), BF16)
    no_prev = jnp.where(jnp.logical_and(first_tile, key_idx < w), -jnp.inf, 0.0)
    units = [(blk, g) for blk in range(rows // w) for g in range(SWA_KV_HEADS)]

    score, values = [], []
    for blk, g in units:
        r0 = blk * w
        ks = slice(g * SWA_KV_LANES, (g + 1) * SWA_KV_LANES)
        k2 = k_all[r0:r0 + 2 * w, ks]
        v2 = v_all[r0:r0 + 2 * w, ks]
        k_bd = jnp.concatenate([jnp.where(left, k2, zero), jnp.where(left, zero, k2)], axis=0)
        values.append(jnp.concatenate([jnp.where(left, v2, zero), jnp.where(left, zero, v2)], axis=0))
        pairs = []
        for p in range(2):
            c0 = (4 * g + 2 * p) * hd
            qp = q_ref[0, r0:r0 + w, c0:c0 + 2 * hd].astype(F32)
            sq = qp * qp
            s_l = jnp.sum(jnp.where(left_q, sq, 0.0), axis=-1, keepdims=True)
            s_r = jnp.sum(jnp.where(left_q, 0.0, sq), axis=-1, keepdims=True)
            ms = jnp.where(left_q, s_l, s_r) * (1.0 / hd)
            pairs.append((qp * lax.rsqrt(ms + NORM_EPS) * qgain * (hd ** -0.5)).astype(BF16))
        qs = jnp.concatenate(pairs, axis=0)
        score.append(_dot_nt(qs, k_bd))

    probs, sinks = [], []
    for (blk, g), s in zip(units, score):
        e_rows, sink_rows = [], []
        for p in range(2):
            halves = []
            for side in range(2):
                hq = 4 * g + 2 * p + side
                logit = s[p * w:(p + 1) * w, side * 2 * w:(side + 1) * 2 * w] + bias_ref[hq]
                if blk == 0:
                    logit = logit + no_prev
                sink = sink_ref[hq]
                m = jnp.maximum(jnp.max(logit, axis=-1, keepdims=True), sink)
                halves.append((jnp.exp(logit - m), jnp.exp(sink - m)))
            e_rows.append(jnp.concatenate([halves[0][0], halves[1][0]], axis=1).astype(BF16))
            sink_rows.append(jnp.where(left_q, halves[0][1], halves[1][1]))
        probs.append(jnp.concatenate(e_rows, axis=0))
        sinks.append(jnp.concatenate(sink_rows, axis=0))

    ones_bd = jnp.concatenate([jnp.where(left, 1.0, 0.0), jnp.where(left, 0.0, 1.0)], axis=0).astype(BF16)
    for (blk, g), e2, v_bd, sink_term in zip(units, probs, values, sinks):
        pvd = _dot(e2, jnp.concatenate([v_bd, ones_bd], axis=1))
        pv = pvd[:, :SWA_KV_LANES] / (pvd[:, SWA_KV_LANES:] + sink_term)
        r0 = blk * w
        for p in range(2):
            c0 = (4 * g + 2 * p) * hd
            attn_ref[r0:r0 + w, c0:c0 + 2 * hd] = pv[p * w:(p + 1) * w].astype(BF16)

    h3 = h2_ref[0] + _dot(attn_ref[...], wo_ref[...])
    h3_ref[0] = h3
    hn2 = h3 * _rms_scale(h3) * fgain_ref[...]
    _store_token_rows(hn2_ref, hn2)
    hnprev_ref[...] = hn2


def _attn_router(q, k2, v2, h2, rel_bias, sink, q_gain, w_o, fgain, w_router):
    bsz, seq, d = h2.shape
    rows = min(ATTN_ROWS, seq)
    w = SWA_WINDOW
    nk = SWA_KV_HEADS * SWA_KV_LANES
    per = rows // w
    nj = seq // rows
    n_tiles = bsz * nj
    bucket = jnp.asarray(_rel_bucket_table())
    qgain2 = jnp.concatenate([q_gain, q_gain]).reshape(1, SWA_KV_LANES)
    wr = jnp.zeros((d, ROUTE_LANES), F32).at[:, :N_EXPERTS].set(w_router)
    wr_hi = wr.astype(BF16)
    wr_lo = (wr - wr_hi.astype(F32)).astype(BF16)
    wr2 = jnp.stack([wr_hi, wr_lo])
    cur = lambda s: jnp.minimum(s, n_tiles - 1)
    tile = lambda width: pl.BlockSpec((1, rows, width), lambda s, *_: (cur(s) // nj, cur(s) % nj, 0))
    prev = pl.BlockSpec(
        (1, w, nk), lambda s, *_: (cur(s) // nj, jnp.maximum((cur(s) % nj) * per - 1, 0), 0))
    routed = lambda s: jnp.maximum(s - 1, 0)
    const = lambda shape: pl.BlockSpec(shape, lambda s, *_: (0,) * len(shape),
                                       pipeline_mode=pl.Buffered(1))
    return pl.pallas_call(
        functools.partial(_attn_router_kernel, nj),
        out_shape=(jax.ShapeDtypeStruct((bsz, seq, d), F32),
                   jax.ShapeDtypeStruct((bsz * seq * ROW_CHUNKS, LANES), F32),
                   jax.ShapeDtypeStruct((bsz, seq, ROUTE_COLS), F32)),
        grid_spec=pltpu.PrefetchScalarGridSpec(
            num_scalar_prefetch=2,
            grid=(n_tiles + 1,),
            in_specs=[tile(d), tile(nk), prev, tile(nk), prev, tile(d),
                      const((w, 2 * w)), const((1, SWA_KV_LANES)), const((d, d)), const((1, d)),
                      const((2, d, ROUTE_LANES))],
            out_specs=(tile(d),
                       pl.BlockSpec((rows * ROW_CHUNKS, LANES), lambda s, *_: (cur(s), 0)),
                       pl.BlockSpec((1, rows, ROUTE_COLS),
                                    lambda s, *_: (routed(s) // nj, routed(s) % nj, 0))),
            scratch_shapes=[pltpu.VMEM((SWA_Q_HEADS, w, 2 * w), F32), pltpu.VMEM((rows, d), BF16),
                            pltpu.VMEM((rows, d), F32)],
        ),
        compiler_params=pltpu.CompilerParams(
            dimension_semantics=("arbitrary",), vmem_limit_bytes=VMEM_LIMIT_BYTES),
        name="attn_router",
    )(rel_bias.reshape(-1), sink, q, k2, k2, v2, v2, h2, bucket, qgain2, w_o, fgain.reshape(1, d), wr2)


TOP_K = 2
D_FF_EXPERT = 3584
MOE_TILE = 512
MOE_FF_SUB = 512
RANK_BLOCK = 1024
DISPATCH_BLOCK = 2048
COMBINE_BLOCK = 1024
ZERO_ROWS = 256
RANK_SUBLANES = 16
DMA_PRIORITIES = 2


def _rank_kernel(e_ref, tri_ref, rank_ref, cnt_ref, carry_ref):
    bt = e_ref.shape[1]

    @pl.when(pl.program_id(0) == 0)
    def _():
        carry_ref[...] = jnp.zeros_like(carry_ref)

    ex = lax.broadcasted_iota(jnp.int32, (RANK_SUBLANES, bt), 0)
    oh0 = ex == e_ref[0:1, :]
    oh1 = ex == e_ref[1:2, :]
    both = oh0.astype(F32) + oh1.astype(F32)
    pos = _dot(both.astype(BF16), tri_ref[...]) + carry_ref[:, 0:1]
    rank_ref[0:1, :] = jnp.sum(jnp.where(oh0, pos, 0.0), axis=0, keepdims=True).astype(jnp.int32)
    rank_ref[1:2, :] = jnp.sum(jnp.where(oh1, pos, 0.0), axis=0, keepdims=True).astype(jnp.int32)
    carry_ref[...] = carry_ref[...] + jnp.sum(both, axis=1, keepdims=True)
    cnt_ref[...] = carry_ref[...]


def _route_ranks(experts):
    _, t = experts.shape
    bt = min(RANK_BLOCK, t)
    tri = jnp.asarray(np.triu(np.ones((bt, bt), np.float32), k=1), dtype=BF16)
    rank, cnt = pl.pallas_call(
        _rank_kernel,
        out_shape=(jax.ShapeDtypeStruct((TOP_K, t), jnp.int32),
                   jax.ShapeDtypeStruct((RANK_SUBLANES, LANES), F32)),
        grid=(t // bt,),
        in_specs=[pl.BlockSpec((TOP_K, bt), lambda i: (0, i)), _const_spec((bt, bt))],
        out_specs=(pl.BlockSpec((TOP_K, bt), lambda i: (0, i)),
                   pl.BlockSpec((RANK_SUBLANES, LANES), lambda i: (0, 0))),
        scratch_shapes=[pltpu.VMEM((RANK_SUBLANES, LANES), F32)],
        compiler_params=pltpu.CompilerParams(dimension_semantics=("arbitrary",)),
        name="route_rank",
    )(experts, tri)
    return rank, cnt[:N_EXPERTS, 0].astype(jnp.int32)


def _row_copy(src, src_token, dst, dst_token, sem):
    return pltpu.make_async_copy(_token_row_slice(src, src_token), _token_row_slice(dst, dst_token), sem)


def _dispatch_kernel(start_ref, zrow_ref, nz_ref, hn_ref, e_ref, rank_ref, xs_ref, dest_ref,
                     dvm_ref, dsm_ref, zero_ref, sem_idx, sem_rows, sem_zero):
    bt = e_ref.shape[1]
    zero_copies = MOE_TILE // ZERO_ROWS

    def zero_copy(z, c):
        row = pl.multiple_of((zrow_ref[z] + c * ZERO_ROWS) * ROW_CHUNKS, ROW_CHUNKS)
        return pltpu.make_async_copy(zero_ref, xs_ref.at[pl.ds(row, ZERO_ROWS * ROW_CHUNKS)], sem_zero)

    @pl.when(pl.program_id(0) == 0)
    def _():
        zero_ref[...] = jnp.zeros_like(zero_ref)

        def start(z, carry):
            for c in range(zero_copies):
                zero_copy(z, c).start()
            return carry

        def wait(z, carry):
            for c in range(zero_copies):
                zero_copy(z, c).wait()
            return carry

        lax.fori_loop(0, nz_ref[0], start, 0)
        lax.fori_loop(0, nz_ref[0], wait, 0)

    e = e_ref[...]
    base = jnp.zeros_like(e)
    for x in range(N_EXPERTS):
        base = jnp.where(e == x, start_ref[x], base)
    dest = base + rank_ref[...]
    dest_ref[...] = dest
    dvm_ref[...] = dest
    idx_copy = pltpu.make_async_copy(dvm_ref, dsm_ref, sem_idx)
    idx_copy.start()
    idx_copy.wait()

    def issue(t, carry):
        for c in range(TOP_K):
            _row_copy(hn_ref, t, xs_ref, dsm_ref[c, t], sem_rows).start(priority=c % DMA_PRIORITIES)
        return carry

    lax.fori_loop(0, bt, issue, 0, unroll=8)
    for c in range(TOP_K):
        pltpu.make_async_copy(hn_ref, xs_ref.at[pl.ds(0, bt * ROW_CHUNKS)], sem_rows).wait()


def _dispatch(hn, experts, rank, start, zrows, nz, m_pad):
    t = hn.shape[0] // ROW_CHUNKS
    bt = min(DISPATCH_BLOCK, t)
    blk = lambda i, *_: (0, i)
    return pl.pallas_call(
        _dispatch_kernel,
        out_shape=(jax.ShapeDtypeStruct((m_pad * ROW_CHUNKS, LANES), F32),
                   jax.ShapeDtypeStruct((TOP_K, t), jnp.int32)),
        grid_spec=pltpu.PrefetchScalarGridSpec(
            num_scalar_prefetch=3,
            grid=(t // bt,),
            in_specs=[pl.BlockSpec((bt * ROW_CHUNKS, LANES), lambda i, *_: (i, 0)),
                      pl.BlockSpec((TOP_K, bt), blk), pl.BlockSpec((TOP_K, bt), blk)],
            out_specs=(pl.BlockSpec(memory_space=pl.ANY), pl.BlockSpec((TOP_K, bt), blk)),
            scratch_shapes=[pltpu.VMEM((TOP_K, bt), jnp.int32), pltpu.SMEM((TOP_K, bt), jnp.int32),
                            pltpu.VMEM((ZERO_ROWS * ROW_CHUNKS, LANES), F32),
                            pltpu.SemaphoreType.DMA, pltpu.SemaphoreType.DMA, pltpu.SemaphoreType.DMA],
        ),
        compiler_params=pltpu.CompilerParams(dimension_semantics=("arbitrary",)),
        name="moe_dispatch",
    )(start, zrows, nz, hn, experts, rank)


def _expert_kernel(te_ref, tv_ref, x_ref, wg_ref, wu_ref, wd_ref, out_ref):
    valid = tv_ref[pl.program_id(0)] == 1

    @pl.when(valid)
    def _():
        xb = _load_token_rows(x_ref, MOE_TILE).astype(BF16)
        acc = jnp.zeros((MOE_TILE, D_MODEL), F32)
        for j in range(D_FF_EXPERT // MOE_FF_SUB):
            sl = slice(j * MOE_FF_SUB, (j + 1) * MOE_FF_SUB)
            gate = _dot(xb, wg_ref[0, :, sl])
            up = _dot(xb, wu_ref[0, :, sl])
            hid = (gate * _sigmoid(gate) * up).astype(BF16)
            acc = acc + _dot(hid, wd_ref[0, sl, :])
        _store_token_rows(out_ref, acc)

    @pl.when(jnp.logical_not(valid))
    def _():
        out_ref[...] = jnp.zeros_like(out_ref)


def _experts(xs, tile_expert, tile_valid, wg, wu, wd):
    d = D_MODEL
    row_block = pl.BlockSpec((MOE_TILE * ROW_CHUNKS, LANES), lambda i, te, tv: (i, 0))
    weights = lambda shape: pl.BlockSpec((1,) + shape, lambda i, te, tv: (te[i], 0, 0),
                                         pipeline_mode=pl.Buffered(1))
    return pl.pallas_call(
        _expert_kernel,
        out_shape=jax.ShapeDtypeStruct(xs.shape, F32),
        grid_spec=pltpu.PrefetchScalarGridSpec(
            num_scalar_prefetch=2,
            grid=(xs.shape[0] // (MOE_TILE * ROW_CHUNKS),),
            in_specs=[row_block, weights((d, D_FF_EXPERT)), weights((d, D_FF_EXPERT)),
                      weights((D_FF_EXPERT, d))],
            out_specs=row_block,
        ),
        compiler_params=pltpu.CompilerParams(
            dimension_semantics=("arbitrary",), vmem_limit_bytes=VMEM_LIMIT_BYTES),
        name="moe_experts",
    )(tile_expert, tile_valid, xs, wg, wu, wd)


def _combine_kernel(h_ref, route_ref, dest_ref, dest_next_ref, yb_ref, out_ref,
                    dsm_ref, ybuf_ref, sem_idx, sem_rows):
    bt = h_ref.shape[0]
    i = pl.program_id(0)
    slot = i % 2

    def start_gathers(step_dest_ref, s):
        idx_copy = pltpu.make_async_copy(step_dest_ref, dsm_ref.at[s], sem_idx)
        idx_copy.start()
        idx_copy.wait()

        def issue(t, carry):
            for c in range(TOP_K):
                _row_copy(yb_ref, dsm_ref[s, c, t], ybuf_ref.at[s, c], t, sem_rows.at[s]).start(
                    priority=c % DMA_PRIORITIES)
            return carry

        lax.fori_loop(0, bt, issue, 0, unroll=8)

    @pl.when(i == 0)
    def _():
        start_gathers(dest_ref, 0)

    @pl.when(i + 1 < pl.num_programs(0))
    def _():
        start_gathers(dest_next_ref, 1 - slot)

    for c in range(TOP_K):
        pltpu.make_async_copy(yb_ref.at[pl.ds(0, bt * ROW_CHUNKS)], ybuf_ref.at[slot, c],
                              sem_rows.at[slot]).wait()
    route = route_ref[...]
    y0 = _load_token_rows(ybuf_ref, bt, (slot, 0))
    y1 = _load_token_rows(ybuf_ref, bt, (slot, 1))
    out_ref[...] = h_ref[...] + route[:, 2:3] * y0 + route[:, 3:4] * y1


def _combine(h3, route, dest, yb):
    t, d = h3.shape
    bt = min(COMBINE_BLOCK, t)
    n = t // bt
    return pl.pallas_call(
        _combine_kernel,
        out_shape=jax.ShapeDtypeStruct((t, d), F32),
        grid=(n,),
        in_specs=[pl.BlockSpec((bt, d), lambda i: (i, 0)),
                  pl.BlockSpec((bt, ROUTE_COLS), lambda i: (i, 0)),
                  pl.BlockSpec((TOP_K, bt), lambda i: (0, i)),
                  pl.BlockSpec((TOP_K, bt), lambda i: (0, jnp.minimum(i + 1, n - 1))),
                  pl.BlockSpec(memory_space=pl.ANY)],
        out_specs=pl.BlockSpec((bt, d), lambda i: (i, 0)),
        scratch_shapes=[pltpu.SMEM((2, TOP_K, bt), jnp.int32),
                        pltpu.VMEM((2, TOP_K, bt * ROW_CHUNKS, LANES), F32),
                        pltpu.SemaphoreType.DMA, pltpu.SemaphoreType.DMA((2,))],
        compiler_params=pltpu.CompilerParams(
            dimension_semantics=("arbitrary",), vmem_limit_bytes=VMEM_LIMIT_BYTES),
        name="moe_combine",
    )(h3, route, dest, dest, yb)


def _moe_layer(h3, hn2, route, wg, wu, wd):
    t, d = h3.shape
    n_tiles = (t * TOP_K) // MOE_TILE + N_EXPERTS
    m_pad = n_tiles * MOE_TILE
    experts = route[:, :TOP_K].astype(jnp.int32).T
    rank, counts = _route_ranks(experts)
    tiles_per = (counts + MOE_TILE - 1) // MOE_TILE
    tile_end = jnp.cumsum(tiles_per)
    start = ((tile_end - tiles_per) * MOE_TILE).astype(jnp.int32)
    tile_ids = jnp.arange(n_tiles, dtype=jnp.int32)
    tile_valid = (tile_ids < tile_end[-1]).astype(jnp.int32)
    last_valid = jnp.maximum(tile_end[-1] - 1, 0)
    tile_expert = jnp.minimum(
        jnp.sum(jnp.minimum(tile_ids, last_valid)[:, None] >= tile_end[None, :], axis=1),
        N_EXPERTS - 1).astype(jnp.int32)
    has_pad = (counts % MOE_TILE) != 0
    pad_tile = jnp.where(has_pad, tile_end - 1, n_tiles)
    zmask = jnp.zeros((n_tiles + 1,), jnp.int32).at[pad_tile].set(1)[:n_tiles]
    zmask = jnp.maximum(zmask, 1 - tile_valid)
    nz = jnp.sum(zmask).astype(jnp.int32).reshape(1)
    zorder = jnp.argsort(1 - zmask, stable=True).astype(jnp.int32)
    zrows = (zorder[:2 * N_EXPERTS] * MOE_TILE).astype(jnp.int32)
    xs, dest = _dispatch(hn2, experts, rank, start, zrows, nz, m_pad)
    yb = _experts(xs, tile_expert, tile_valid, wg, wu, wd)
    return _combine(h3, route, dest, yb)


def _stages(x, hgrn_w_in, hgrn_lb, hgrn_gnorm, hgrn_w_out, swa_w_q, swa_q_gain, swa_sink, swa_w_o, kv_norm, kv_w, k_gain, rel_bias, attn_norm, ffn_norm, ffn_w_gate, ffn_w_up, ffn_w_down, moe_router, moe_w_gate, moe_w_up, moe_w_down):
    bsz, seq, d = x.shape
    t = bsz * seq
    bf = lambda w: w.astype(BF16)
    h1 = _hgrn_layer(x, attn_norm[0], bf(hgrn_w_in[0]), hgrn_lb, hgrn_gnorm[0], bf(hgrn_w_out[0]))
    h2, q, k2, v2 = _ffn_kvq(h1.reshape(t, d), ffn_norm[0], bf(ffn_w_gate[0]), bf(ffn_w_up[0]),
                             bf(ffn_w_down[0]), kv_norm, kv_w, k_gain, attn_norm[1], bf(swa_w_q[0]))
    nk = SWA_KV_HEADS * SWA_KV_LANES
    h3, hn2, route = _attn_router(q.reshape(bsz, seq, d), k2.reshape(bsz, seq, nk),
                                  v2.reshape(bsz, seq, nk), h2.reshape(bsz, seq, d), rel_bias,
                                  swa_sink[0], swa_q_gain[0], bf(swa_w_o[0]), ffn_norm[1],
                                  moe_router[0])
    h4 = _moe_layer(h3.reshape(t, d), hn2, route.reshape(t, ROUTE_COLS),
                    bf(moe_w_gate[0]), bf(moe_w_up[0]), bf(moe_w_down[0]))
    return {"h1": h1, "h2": h2.reshape(bsz, seq, d), "h3": h3, "h4": h4.reshape(bsz, seq, d)}


def kernel(x, hgrn_w_in, hgrn_lb, hgrn_gnorm, hgrn_w_out, swa_w_q, swa_q_gain, swa_sink, swa_w_o, kv_norm, kv_w, k_gain, rel_bias, attn_norm, ffn_norm, ffn_w_gate, ffn_w_up, ffn_w_down, moe_router, moe_w_gate, moe_w_up, moe_w_down):
    return _stages(x, hgrn_w_in, hgrn_lb, hgrn_gnorm, hgrn_w_out, swa_w_q, swa_q_gain, swa_sink, swa_w_o, kv_norm, kv_w, k_gain, rel_bias, attn_norm, ffn_norm, ffn_w_gate, ffn_w_up, ffn_w_down, moe_router, moe_w_gate, moe_w_up, moe_w_down)["h4"]
```

```python
import functools

import numpy as np
import jax
import jax.numpy as jnp
from jax import lax
from jax.experimental import pallas as pl
from jax.experimental.pallas import tpu as pltpu

F32 = jnp.float32
BF16 = jnp.bfloat16

D_MODEL = 1024
NORM_EPS = 1e-6
LOG2_E = 1.4426950408889634

LANES = 128
MXU_WIDTH = 256

HG_HEADS = 8
HG_DK = 128
HG_DV = 128
HG_CHUNK = 128
HG_LEVELS = 7
HG_STEP_CHUNKS = 4
HG_GROUP_HEADS = 8

VMEM_LIMIT_BYTES = 56 * 1024 * 1024


def _dot(a, b):
    return jnp.dot(a, b, preferred_element_type=F32)


def _dot_nt(a, b):
    return lax.dot_general(a, b, (((1,), (1,)), ((), ())), preferred_element_type=F32)


def _dot_tn(a, b):
    return lax.dot_general(a, b, (((0,), (0,)), ((), ())), preferred_element_type=F32)


def _rms_scale(x):
    return lax.rsqrt(jnp.mean(x * x, axis=-1, keepdims=True) + NORM_EPS)


def _sigmoid(x):
    return 1.0 / (1.0 + jnp.exp(-x))


ROW_CHUNKS = D_MODEL // LANES


def _load_token_rows(ref, n, lead=()):
    return jnp.concatenate(
        [ref[lead + (pl.ds(c, n, stride=ROW_CHUNKS), slice(None))] for c in range(ROW_CHUNKS)], axis=1)


def _store_token_rows(ref, x):
    n = x.shape[0]
    for c in range(ROW_CHUNKS):
        ref[pl.ds(c, n, stride=ROW_CHUNKS), :] = x[:, c * LANES:(c + 1) * LANES]


def _token_row_slice(ref, token):
    return ref.at[pl.ds(pl.multiple_of(token * ROW_CHUNKS, ROW_CHUNKS), ROW_CHUNKS)]


def _const_spec(shape):
    nd = len(shape)
    return pl.BlockSpec(shape, lambda *_: (0,) * nd, pipeline_mode=pl.Buffered(1))


def _hgrn_level_log_decay(b, level):
    c = HG_CHUNK
    m = 1 << level
    parts = []
    if 2 * m >= 8:
        for start in range(0, c, 2 * m):
            mid = start + m - 1
            parts.append(b[start:start + 2 * m] - b[mid:mid + 1])
    else:
        first_block = lax.broadcasted_iota(jnp.int32, (8, b.shape[1]), 0) < 4
        for start in range(0, c, 8):
            b_mid = jnp.where(first_block, b[start + 1:start + 2], b[start + 5:start + 6])
            parts.append(b[start:start + 8] - b_mid)
    return jnp.concatenate(parts, axis=0)


def _hgrn_kernel(x_ref, gain_ref, win_ref, lbp_ref, tsum_ref, gnorm_ref, wout_ref,
                 out_ref, st_ref, o_ref):
    c = HG_CHUNK

    @pl.when(pl.program_id(1) == 0)
    def _():
        st_ref[...] = jnp.zeros_like(st_ref)

    chunks = [slice(ci * c, (ci + 1) * c) for ci in range(HG_STEP_CHUNKS)]

    lbp = lbp_ref[...]
    lbe = jnp.exp(lbp - jnp.max(lbp, axis=0, keepdims=True))
    lb = lbe[0:1] / jnp.sum(lbe, axis=0, keepdims=True)

    x = x_ref[0]
    hn = (x * _rms_scale(x) * gain_ref[...]).astype(BF16)

    ti = lax.broadcasted_iota(jnp.int32, (c, c), 0)
    si = lax.broadcasted_iota(jnp.int32, (c, c), 1)
    diag = ti == si
    xor = ti ^ si
    lv_mask = [((xor >> l) == 1) & (((ti >> l) & 1) == 1) for l in range(HG_LEVELS)]
    keep = [jnp.where(m, 1.0, 0.0) for m in [diag] + lv_mask]

    row = lax.broadcasted_iota(jnp.int32, (c, HG_DK), 0)
    upper_rows = [((row >> l) & 1) == 1 for l in range(HG_LEVELS)]
    signs = [jnp.where(u, LOG2_E, -LOG2_E) for u in upper_rows]

    gw = HG_GROUP_HEADS * HG_DK
    for p in range(HG_HEADS // HG_GROUP_HEADS):
        cs = slice(p * gw, (p + 1) * gw)
        sec = [_dot(hn, win_ref[:, i * D_MODEL + p * gw:i * D_MODEL + (p + 1) * gw]) for i in range(4)]
        f = lb[:, cs] + (1.0 - lb[:, cs]) * _sigmoid(sec[1])
        g = jnp.log(f)
        g_hi = g.astype(BF16)
        g_lo = (g - g_hi.astype(F32)).astype(BF16)
        b = [_dot(tsum_ref[...], jnp.concatenate([g_hi[rs], g_lo[rs]], axis=0)) for rs in chunks]

        staged = []
        for ci, rs in enumerate(chunks):
            for hh in range(HG_GROUP_HEADS):
                ls = slice(hh * HG_DK, (hh + 1) * HG_DK)
                q_h = sec[0][rs, ls]
                f_h = f[rs, ls]
                b_h = b[ci][:, ls]
                k_h = 1.0 - f_h
                b_last = b_h[c - 1:c]
                per_level = []
                for l in range(HG_LEVELS):
                    if l == 0:
                        w = jnp.where(upper_rows[0], f_h, 1.0)
                    else:
                        w = jnp.exp2(_hgrn_level_log_decay(b_h, l) * signs[l])
                    per_level.append((jnp.where(upper_rows[l], q_h, k_h) * w).astype(BF16))
                staged.append((rs, p * HG_GROUP_HEADS + hh, dict(
                    levels=per_level, q=q_h.astype(BF16), k=k_h.astype(BF16),
                    qb=(q_h * jnp.exp(b_h)).astype(BF16),
                    khat=(k_h * jnp.exp(b_last - b_h)).astype(BF16),
                    decay=jnp.exp(b_last),
                    v=sec[2][rs, ls].astype(BF16),
                    o_gate=sec[3][rs, ls])))

        scores = []
        for _, _, s in staged:
            a = _dot_nt(s["q"], s["k"]) * keep[0]
            for m, lv in zip(keep[1:], s["levels"]):
                a = a + _dot_nt(lv, lv) * m
            scores.append(a.astype(BF16))

        outs = []
        for (_, h, s), a in zip(staged, scores):
            st = st_ref[h]
            outs.append(_dot(a, s["v"]) + _dot_nt(s["qb"], st.astype(BF16)))
            st_ref[h] = st * s["decay"] + _dot_tn(s["v"], s["khat"])

        for (rs, h, s), o_h in zip(staged, outs):
            o_h = o_h * _rms_scale(o_h) * gnorm_ref[...]
            og = s["o_gate"]
            o_ref[rs, h * HG_DV:(h + 1) * HG_DV] = (o_h * (og * _sigmoid(og))).astype(BF16)

    out_ref[0] = x + _dot(o_ref[...], wout_ref[...])


def _hgrn_layer(x, gain, w_in, lb_param, gnorm, w_out):
    bsz, seq, d = x.shape
    c = HG_CHUNK * HG_STEP_CHUNKS
    tril = np.tril(np.ones((HG_CHUNK, HG_CHUNK), np.float32))
    tsum2 = jnp.asarray(np.concatenate([tril, tril], axis=1), dtype=BF16)
    return pl.pallas_call(
        _hgrn_kernel,
        out_shape=jax.ShapeDtypeStruct((bsz, seq, d), F32),
        grid=(bsz, seq // c),
        in_specs=[
            pl.BlockSpec((1, c, d), lambda b, j: (b, j, 0)),
            _const_spec((1, d)),
            _const_spec((d, 4 * d)),
            _const_spec(lb_param.shape),
            _const_spec(tsum2.shape),
            _const_spec((1, HG_DV)),
            _const_spec((d, d)),
        ],
        out_specs=pl.BlockSpec((1, c, d), lambda b, j: (b, j, 0)),
        scratch_shapes=[
            pltpu.VMEM((HG_HEADS, HG_DV, HG_DK), F32),
            pltpu.VMEM((c, d), BF16),
        ],
        compiler_params=pltpu.CompilerParams(
            dimension_semantics=("arbitrary", "arbitrary"),
            vmem_limit_bytes=VMEM_LIMIT_BYTES),
        name="hgrn_layer",
    )(x, gain.reshape(1, d), w_in, lb_param, tsum2, gnorm.reshape(1, HG_DV), w_out)


D_FF_DENSE = 2816
FFN_FF_SPLITS = tuple(MXU_WIDTH * n for n in (0, 3, 6, 9, 11))
FFN_ROWS = 512

SWA_HEAD_DIM = 64
SWA_Q_HEADS = 16
SWA_KV_HEADS = 4
SWA_WINDOW = 128
SWA_KV_LANES = 2 * SWA_HEAD_DIM


def _ffn_kvq_kernel(h_ref, fgain_ref, wg_ref, wu_ref, wd_ref, kvgain_ref, kvw_ref, kgain_ref,
                    qgain_ref, wq_ref, h2_ref, q_ref, k_ref, v_ref):
    h = h_ref[...]
    hn = (h * _rms_scale(h) * fgain_ref[...]).astype(BF16)
    acc = h
    for c in range(len(FFN_FF_SPLITS) - 1):
        sl = slice(FFN_FF_SPLITS[c], FFN_FF_SPLITS[c + 1])
        gate = _dot(hn, wg_ref[:, sl])
        up = _dot(hn, wu_ref[:, sl])
        hid = (gate * _sigmoid(gate) * up).astype(BF16)
        acc = acc + _dot(hid, wd_ref[sl, :])
    h2_ref[...] = acc
    normed = acc * _rms_scale(acc)
    kv = _dot((normed * kvgain_ref[...]).astype(BF16), kvw_ref[...])
    nk = SWA_KV_HEADS * SWA_KV_LANES
    for g in range(SWA_KV_HEADS):
        sl = slice(g * SWA_KV_LANES, (g + 1) * SWA_KV_LANES)
        kg = kv[:, sl]
        k_ref[:, sl] = (kg * _rms_scale(kg) * kgain_ref[...]).astype(BF16)
    v_ref[...] = kv[:, nk:].astype(BF16)
    q_ref[...] = _dot((normed * qgain_ref[...]).astype(BF16), wq_ref[...]).astype(BF16)


def _dup_kv_columns(w):
    d = w.shape[0]
    w = w.reshape(d, SWA_KV_HEADS, 1, SWA_HEAD_DIM)
    return jnp.broadcast_to(w, (d, SWA_KV_HEADS, 2, SWA_HEAD_DIM)).reshape(d, SWA_KV_HEADS * SWA_KV_LANES)


def _ffn_kvq(h1, fgain, wg, wu, wd, kvgain, kv_w, k_gain, qgain, wq):
    t, d = h1.shape
    rows = min(FFN_ROWS, t)
    nkv = SWA_KV_HEADS * SWA_HEAD_DIM
    kvw2 = jnp.concatenate([_dup_kv_columns(kv_w[:, :nkv]), _dup_kv_columns(kv_w[:, nkv:])],
                           axis=1).astype(BF16)
    kgain2 = jnp.concatenate([k_gain, k_gain]).reshape(1, SWA_KV_LANES)
    nk = SWA_KV_HEADS * SWA_KV_LANES
    row_spec = lambda w: pl.BlockSpec((rows, w), lambda i: (i, 0))
    return pl.pallas_call(
        _ffn_kvq_kernel,
        out_shape=(jax.ShapeDtypeStruct((t, d), F32), jax.ShapeDtypeStruct((t, d), BF16),
                   jax.ShapeDtypeStruct((t, nk), BF16), jax.ShapeDtypeStruct((t, nk), BF16)),
        grid=(t // rows,),
        in_specs=[
            row_spec(d),
            _const_spec((1, d)),
            _const_spec((d, D_FF_DENSE)), _const_spec((d, D_FF_DENSE)), _const_spec((D_FF_DENSE, d)),
            _const_spec((1, d)), _const_spec((d, 2 * nk)), _const_spec((1, SWA_KV_LANES)),
            _const_spec((1, d)), _const_spec((d, d)),
        ],
        out_specs=(row_spec(d), row_spec(d), row_spec(nk), row_spec(nk)),
        compiler_params=pltpu.CompilerParams(
            dimension_semantics=("arbitrary",), vmem_limit_bytes=VMEM_LIMIT_BYTES),
        name="ffn_kvq",
    )(h1, fgain.reshape(1, d), wg, wu, wd, kvgain.reshape(1, d), kvw2, kgain2,
      qgain.reshape(1, d), wq)


ATTN_ROWS = 512
REL_BUCKETS = 32
REL_MAX_DIST = 128
N_EXPERTS = 8
ROUTE_LANES = 128
ROUTE_COLS = 8


def _rel_bucket_table():
    w = SWA_WINDOW
    qi = np.arange(w)[:, None]
    kj = np.arange(2 * w)[None, :]
    dist = qi + w - kj
    in_win = (dist >= 0) & (dist < w)
    n = np.maximum(dist, 0)
    max_exact = REL_BUCKETS // 2
    nf = np.maximum(n, 1).astype(np.float64)
    large = max_exact + (np.log(nf / max_exact) / np.log(REL_MAX_DIST / max_exact)
                         * (REL_BUCKETS - max_exact)).astype(np.int64)
    large = np.minimum(large, REL_BUCKETS - 1)
    bucket = np.where(n < max_exact, n, large)
    return np.where(in_win, bucket, -1).astype(np.int32)


def _attn_router_kernel(tiles_per_seq, relb_ref, sink_ref, q_ref, kc_ref, kp_ref, vc_ref, vp_ref,
                        h2_ref, bucket_ref, qgain_ref, wo_ref, fgain_ref, wr_ref,
                        h3_ref, hn2_ref, route_ref, bias_ref, attn_ref, hnprev_ref):
    w = SWA_WINDOW
    hd = SWA_HEAD_DIM
    rows = q_ref.shape[1]
    step = pl.program_id(0)
    last_tile = pl.num_programs(0) - 2
    first_tile = jnp.minimum(step, last_tile) % tiles_per_seq == 0

    @pl.when(step == 0)
    def _():
        hnprev_ref[...] = jnp.zeros_like(hnprev_ref)
        bucket = bucket_ref[...]
        for h in range(SWA_Q_HEADS):
            plane = jnp.full((w, 2 * w), -jnp.inf, F32)
            for bkt in range(REL_BUCKETS):
                plane = jnp.where(bucket == bkt, relb_ref[bkt * SWA_Q_HEADS + h], plane)
            bias_ref[h] = plane

    hp = hnprev_ref[...]
    hi = hp.astype(BF16)
    lo = (hp - hi.astype(F32)).astype(BF16)
    logits = _dot(hi, wr_ref[0]) + _dot(lo, wr_ref[0]) + _dot(hi, wr_ref[1])
    col = lax.broadcasted_iota(jnp.int32, logits.shape, 1)
    logits = jnp.where(col < N_EXPERTS, logits, -jnp.inf)
    m0 = jnp.max(logits, axis=-1, keepdims=True)
    i0 = jnp.min(jnp.where(logits == m0, col, ROUTE_LANES), axis=-1, keepdims=True)
    rest = jnp.where(col == i0, -jnp.inf, logits)
    m1 = jnp.max(rest, axis=-1, keepdims=True)
    i1 = jnp.min(jnp.where(rest == m1, col, ROUTE_LANES), axis=-1, keepdims=True)
    e1 = jnp.exp(m1 - m0)
    g0 = 1.0 / (1.0 + e1)
    g1 = e1 / (1.0 + e1)
    rc = lax.broadcasted_iota(jnp.int32, (rows, ROUTE_COLS), 1)
    route_ref[0] = jnp.where(rc == 0, i0.astype(F32),
                             jnp.where(rc == 1, i1.astype(F32),
                                       jnp.where(rc == 2, g0, jnp.where(rc == 3, g1, 0.0))))


    lane = lax.broadcasted_iota(jnp.int32, (2 * w, SWA_KV_LANES), 1)
    left = lane < hd
    lane_q = lax.broadcasted_iota(jnp.int32, (w, SWA_KV_LANES), 1)
    left_q = lane_q < hd
    key_idx = lax.broadcasted_iota(jnp.int32, (1, 2 * w), 1)
    qgain = qgain_ref[...]

    k_all = jnp.concatenate([kp_ref[0], kc_ref[0]], axis=0)
    v_all = jnp.concatenate([vp_ref[0], vc_ref[0]], axis=0)
    zero = jnp.zeros((), BF16)
    no_prev = jnp.where(jnp.logical_and(first_tile, key_idx < w), -jnp.inf, 0.0)
    units = [(blk, g) for blk in range(rows // w) for g in range(SWA_KV_HEADS)]

    score, values = [], []
    for blk, g in units:
        r0 = blk * w
        ks = slice(g * SWA_KV_LANES, (g + 1) * SWA_KV_LANES)
        k2 = k_all[r0:r0 + 2 * w, ks]
        v2 = v_all[r0:r0 + 2 * w, ks]
        k_bd = jnp.concatenate([jnp.where(left, k2, zero), jnp.where(left, zero, k2)], axis=0)
        values.append(jnp.concatenate([jnp.where(left, v2, zero), jnp.where(left, zero, v2)], axis=0))
        pairs = []
        for p in range(2):
            c0 = (4 * g + 2 * p) * hd
            qp = q_ref[0, r0:r0 + w, c0:c0 + 2 * hd].astype(F32)
            sq = qp * qp
            s_l = jnp.sum(jnp.where(left_q, sq, 0.0), axis=-1, keepdims=True)
            s_r = jnp.sum(jnp.where(left_q, 0.0, sq), axis=-1, keepdims=True)
            ms = jnp.where(left_q, s_l, s_r) * (1.0 / hd)
            pairs.append((qp * lax.rsqrt(ms + NORM_EPS) * qgain * (hd ** -0.5)).astype(BF16))
        qs = jnp.concatenate(pairs, axis=0)
        score.append(_dot_nt(qs, k_bd))

    probs, sinks = [], []
    for (blk, g), s in zip(units, score):
        e_rows, sink_rows = [], []
        for p in range(2):
            halves = []
            for side in range(2):
                hq = 4 * g + 2 * p + side
                logit = s[p * w:(p + 1) * w, side * 2 * w:(side + 1) * 2 * w] + bias_ref[hq]
                if blk == 0:
                    logit = logit + no_prev
                sink = sink_ref[hq]
                m = jnp.maximum(jnp.max(logit, axis=-1, keepdims=True), sink)
                halves.append((jnp.exp(logit - m), jnp.exp(sink - m)))
            e_rows.append(jnp.concatenate([halves[0][0], halves[1][0]], axis=1).astype(BF16))
            sink_rows.append(jnp.where(left_q, halves[0][1], halves[1][1]))
        probs.append(jnp.concatenate(e_rows, axis=0))
        sinks.append(jnp.concatenate(sink_rows, axis=0))

    ones_bd = jnp.concatenate([jnp.where(left, 1.0, 0.0), jnp.where(left, 0.0, 1.0)], axis=0).astype(BF16)
    for (blk, g), e2, v_bd, sink_term in zip(units, probs, values, sinks):
        pvd = _dot(e2, jnp.concatenate([v_bd, ones_bd], axis=1))
        pv = pvd[:, :SWA_KV_LANES] / (pvd[:, SWA_KV_LANES:] + sink_term)
        r0 = blk * w
        for p in range(2):
            c0 = (4 * g + 2 * p) * hd
            attn_ref[r0:r0 + w, c0:c0 + 2 * hd] = pv[p * w:(p + 1) * w].astype(BF16)

    h3 = h2_ref[0] + _dot(attn_ref[...], wo_ref[...])
    h3_ref[0] = h3
    hn2 = h3 * _rms_scale(h3) * fgain_ref[...]
    _store_token_rows(hn2_ref, hn2)
    hnprev_ref[...] = hn2


def _attn_router(q, k2, v2, h2, rel_bias, sink, q_gain, w_o, fgain, w_router):
    bsz, seq, d = h2.shape
    rows = min(ATTN_ROWS, seq)
    w = SWA_WINDOW
    nk = SWA_KV_HEADS * SWA_KV_LANES
    per = rows // w
    nj = seq // rows
    n_tiles = bsz * nj
    bucket = jnp.asarray(_rel_bucket_table())
    qgain2 = jnp.concatenate([q_gain, q_gain]).reshape(1, SWA_KV_LANES)
    wr = jnp.zeros((d, ROUTE_LANES), F32).at[:, :N_EXPERTS].set(w_router)
    wr_hi = wr.astype(BF16)
    wr_lo = (wr - wr_hi.astype(F32)).astype(BF16)
    wr2 = jnp.stack([wr_hi, wr_lo])
    cur = lambda s: jnp.minimum(s, n_tiles - 1)
    tile = lambda width: pl.BlockSpec((1, rows, width), lambda s, *_: (cur(s) // nj, cur(s) % nj, 0))
    prev = pl.BlockSpec(
        (1, w, nk), lambda s, *_: (cur(s) // nj, jnp.maximum((cur(s) % nj) * per - 1, 0), 0))
    routed = lambda s: jnp.maximum(s - 1, 0)
    const = lambda shape: pl.BlockSpec(shape, lambda s, *_: (0,) * len(shape),
                                       pipeline_mode=pl.Buffered(1))
    return pl.pallas_call(
        functools.partial(_attn_router_kernel, nj),
        out_shape=(jax.ShapeDtypeStruct((bsz, seq, d), F32),
                   jax.ShapeDtypeStruct((bsz * seq * ROW_CHUNKS, LANES), F32),
                   jax.ShapeDtypeStruct((bsz, seq, ROUTE_COLS), F32)),
        grid_spec=pltpu.PrefetchScalarGridSpec(
            num_scalar_prefetch=2,
            grid=(n_tiles + 1,),
            in_specs=[tile(d), tile(nk), prev, tile(nk), prev, tile(d),
                      const((w, 2 * w)), const((1, SWA_KV_LANES)), const((d, d)), const((1, d)),
                      const((2, d, ROUTE_LANES))],
            out_specs=(tile(d),
                       pl.BlockSpec((rows * ROW_CHUNKS, LANES), lambda s, *_: (cur(s), 0)),
                       pl.BlockSpec((1, rows, ROUTE_COLS),
                                    lambda s, *_: (routed(s) // nj, routed(s) % nj, 0))),
            scratch_shapes=[pltpu.VMEM((SWA_Q_HEADS, w, 2 * w), F32), pltpu.VMEM((rows, d), BF16),
                            pltpu.VMEM((rows, d), F32)],
        ),
        compiler_params=pltpu.CompilerParams(
            dimension_semantics=("arbitrary",), vmem_limit_bytes=VMEM_LIMIT_BYTES),
        name="attn_router",
    )(rel_bias.reshape(-1), sink, q, k2, k2, v2, v2, h2, bucket, qgain2, w_o, fgain.reshape(1, d), wr2)


TOP_K = 2
D_FF_EXPERT = 3584
MOE_TILE = 512
MOE_FF_SUB = 512
RANK_BLOCK = 1024
DISPATCH_BLOCK = 2048
COMBINE_BLOCK = 1024
ZERO_ROWS = 256
RANK_SUBLANES = 16
DMA_PRIORITIES = 2


def _rank_kernel(e_ref, tri_ref, rank_ref, cnt_ref, carry_ref):
    bt = e_ref.shape[1]

    @pl.when(pl.program_id(0) == 0)
    def _():
        carry_ref[...] = jnp.zeros_like(carry_ref)

    ex = lax.broadcasted_iota(jnp.int32, (RANK_SUBLANES, bt), 0)
    oh0 = ex == e_ref[0:1, :]
    oh1 = ex == e_ref[1:2, :]
    both = oh0.astype(F32) + oh1.astype(F32)
    pos = _dot(both.astype(BF16), tri_ref[...]) + carry_ref[:, 0:1]
    rank_ref[0:1, :] = jnp.sum(jnp.where(oh0, pos, 0.0), axis=0, keepdims=True).astype(jnp.int32)
    rank_ref[1:2, :] = jnp.sum(jnp.where(oh1, pos, 0.0), axis=0, keepdims=True).astype(jnp.int32)
    carry_ref[...] = carry_ref[...] + jnp.sum(both, axis=1, keepdims=True)
    cnt_ref[...] = carry_ref[...]


def _route_ranks(experts):
    _, t = experts.shape
    bt = min(RANK_BLOCK, t)
    tri = jnp.asarray(np.triu(np.ones((bt, bt), np.float32), k=1), dtype=BF16)
    rank, cnt = pl.pallas_call(
        _rank_kernel,
        out_shape=(jax.ShapeDtypeStruct((TOP_K, t), jnp.int32),
                   jax.ShapeDtypeStruct((RANK_SUBLANES, LANES), F32)),
        grid=(t // bt,),
        in_specs=[pl.BlockSpec((TOP_K, bt), lambda i: (0, i)), _const_spec((bt, bt))],
        out_specs=(pl.BlockSpec((TOP_K, bt), lambda i: (0, i)),
                   pl.BlockSpec((RANK_SUBLANES, LANES), lambda i: (0, 0))),
        scratch_shapes=[pltpu.VMEM((RANK_SUBLANES, LANES), F32)],
        compiler_params=pltpu.CompilerParams(dimension_semantics=("arbitrary",)),
        name="route_rank",
    )(experts, tri)
    return rank, cnt[:N_EXPERTS, 0].astype(jnp.int32)


def _row_copy(src, src_token, dst, dst_token, sem):
    return pltpu.make_async_copy(_token_row_slice(src, src_token), _token_row_slice(dst, dst_token), sem)


def _dispatch_kernel(start_ref, zrow_ref, nz_ref, hn_ref, e_ref, rank_ref, xs_ref, dest_ref,
                     dvm_ref, dsm_ref, zero_ref, sem_idx, sem_rows, sem_zero):
    bt = e_ref.shape[1]
    zero_copies = MOE_TILE // ZERO_ROWS

    def zero_copy(z, c):
        row = pl.multiple_of((zrow_ref[z] + c * ZERO_ROWS) * ROW_CHUNKS, ROW_CHUNKS)
        return pltpu.make_async_copy(zero_ref, xs_ref.at[pl.ds(row, ZERO_ROWS * ROW_CHUNKS)], sem_zero)

    @pl.when(pl.program_id(0) == 0)
    def _():
        zero_ref[...] = jnp.zeros_like(zero_ref)

        def start(z, carry):
            for c in range(zero_copies):
                zero_copy(z, c).start()
            return carry

        def wait(z, carry):
            for c in range(zero_copies):
                zero_copy(z, c).wait()
            return carry

        lax.fori_loop(0, nz_ref[0], start, 0)
        lax.fori_loop(0, nz_ref[0], wait, 0)

    e = e_ref[...]
    base = jnp.zeros_like(e)
    for x in range(N_EXPERTS):
        base = jnp.where(e == x, start_ref[x], base)
    dest = base + rank_ref[...]
    dest_ref[...] = dest
    dvm_ref[...] = dest
    idx_copy = pltpu.make_async_copy(dvm_ref, dsm_ref, sem_idx)
    idx_copy.start()
    idx_copy.wait()

    def issue(t, carry):
        for c in range(TOP_K):
            _row_copy(hn_ref, t, xs_ref, dsm_ref[c, t], sem_rows).start(priority=c % DMA_PRIORITIES)
        return carry

    lax.fori_loop(0, bt, issue, 0, unroll=8)
    for c in range(TOP_K):
        pltpu.make_async_copy(hn_ref, xs_ref.at[pl.ds(0, bt * ROW_CHUNKS)], sem_rows).wait()


def _dispatch(hn, experts, rank, start, zrows, nz, m_pad):
    t = hn.shape[0] // ROW_CHUNKS
    bt = min(DISPATCH_BLOCK, t)
    blk = lambda i, *_: (0, i)
    return pl.pallas_call(
        _dispatch_kernel,
        out_shape=(jax.ShapeDtypeStruct((m_pad * ROW_CHUNKS, LANES), F32),
                   jax.ShapeDtypeStruct((TOP_K, t), jnp.int32)),
        grid_spec=pltpu.PrefetchScalarGridSpec(
            num_scalar_prefetch=3,
            grid=(t // bt,),
            in_specs=[pl.BlockSpec((bt * ROW_CHUNKS, LANES), lambda i, *_: (i, 0)),
                      pl.BlockSpec((TOP_K, bt), blk), pl.BlockSpec((TOP_K, bt), blk)],
            out_specs=(pl.BlockSpec(memory_space=pl.ANY), pl.BlockSpec((TOP_K, bt), blk)),
            scratch_shapes=[pltpu.VMEM((TOP_K, bt), jnp.int32), pltpu.SMEM((TOP_K, bt), jnp.int32),
                            pltpu.VMEM((ZERO_ROWS * ROW_CHUNKS, LANES), F32),
                            pltpu.SemaphoreType.DMA, pltpu.SemaphoreType.DMA, pltpu.SemaphoreType.DMA],
        ),
        compiler_params=pltpu.CompilerParams(dimension_semantics=("arbitrary",)),
        name="moe_dispatch",
    )(start, zrows, nz, hn, experts, rank)


def _expert_kernel(te_ref, tv_ref, x_ref, wg_ref, wu_ref, wd_ref, out_ref):
    valid = tv_ref[pl.program_id(0)] == 1

    @pl.when(valid)
    def _():
        xb = _load_token_rows(x_ref, MOE_TILE).astype(BF16)
        acc = jnp.zeros((MOE_TILE, D_MODEL), F32)
        for j in range(D_FF_EXPERT // MOE_FF_SUB):
            sl = slice(j * MOE_FF_SUB, (j + 1) * MOE_FF_SUB)
            gate = _dot(xb, wg_ref[0, :, sl])
            up = _dot(xb, wu_ref[0, :, sl])
            hid = (gate * _sigmoid(gate) * up).astype(BF16)
            acc = acc + _dot(hid, wd_ref[0, sl, :])
        _store_token_rows(out_ref, acc)

    @pl.when(jnp.logical_not(valid))
    def _():
        out_ref[...] = jnp.zeros_like(out_ref)


def _experts(xs, tile_expert, tile_valid, wg, wu, wd):
    d = D_MODEL
    row_block = pl.BlockSpec((MOE_TILE * ROW_CHUNKS, LANES), lambda i, te, tv: (i, 0))
    weights = lambda shape: pl.BlockSpec((1,) + shape, lambda i, te, tv: (te[i], 0, 0),
                                         pipeline_mode=pl.Buffered(1))
    return pl.pallas_call(
        _expert_kernel,
        out_shape=jax.ShapeDtypeStruct(xs.shape, F32),
        grid_spec=pltpu.PrefetchScalarGridSpec(
            num_scalar_prefetch=2,
            grid=(xs.shape[0] // (MOE_TILE * ROW_CHUNKS),),
            in_specs=[row_block, weights((d, D_FF_EXPERT)), weights((d, D_FF_EXPERT)),
                      weights((D_FF_EXPERT, d))],
            out_specs=row_block,
        ),
        compiler_params=pltpu.CompilerParams(
            dimension_semantics=("arbitrary",), vmem_limit_bytes=VMEM_LIMIT_BYTES),
        name="moe_experts",
    )(tile_expert, tile_valid, xs, wg, wu, wd)


def _combine_kernel(h_ref, route_ref, dest_ref, dest_next_ref, yb_ref, out_ref,
                    dsm_ref, ybuf_ref, sem_idx, sem_rows):
    bt = h_ref.shape[0]
    i = pl.program_id(0)
    slot = i % 2

    def start_gathers(step_dest_ref, s):
        idx_copy = pltpu.make_async_copy(step_dest_ref, dsm_ref.at[s], sem_idx)
        idx_copy.start()
        idx_copy.wait()

        def issue(t, carry):
            for c in range(TOP_K):
                _row_copy(yb_ref, dsm_ref[s, c, t], ybuf_ref.at[s, c], t, sem_rows.at[s]).start(
                    priority=c % DMA_PRIORITIES)
            return carry

        lax.fori_loop(0, bt, issue, 0, unroll=8)

    @pl.when(i == 0)
    def _():
        start_gathers(dest_ref, 0)

    @pl.when(i + 1 < pl.num_programs(0))
    def _():
        start_gathers(dest_next_ref, 1 - slot)

    for c in range(TOP_K):
        pltpu.make_async_copy(yb_ref.at[pl.ds(0, bt * ROW_CHUNKS)], ybuf_ref.at[slot, c],
                              sem_rows.at[slot]).wait()
    route = route_ref[...]
    y0 = _load_token_rows(ybuf_ref, bt, (slot, 0))
    y1 = _load_token_rows(ybuf_ref, bt, (slot, 1))
    out_ref[...] = h_ref[...] + route[:, 2:3] * y0 + route[:, 3:4] * y1


def _combine(h3, route, dest, yb):
    t, d = h3.shape
    bt = min(COMBINE_BLOCK, t)
    n = t // bt
    return pl.pallas_call(
        _combine_kernel,
        out_shape=jax.ShapeDtypeStruct((t, d), F32),
        grid=(n,),
        in_specs=[pl.BlockSpec((bt, d), lambda i: (i, 0)),
                  pl.BlockSpec((bt, ROUTE_COLS), lambda i: (i, 0)),
                  pl.BlockSpec((TOP_K, bt), lambda i: (0, i)),
                  pl.BlockSpec((TOP_K, bt), lambda i: (0, jnp.minimum(i + 1, n - 1))),
                  pl.BlockSpec(memory_space=pl.ANY)],
        out_specs=pl.BlockSpec((bt, d), lambda i: (i, 0)),
        scratch_shapes=[pltpu.SMEM((2, TOP_K, bt), jnp.int32),
                        pltpu.VMEM((2, TOP_K, bt * ROW_CHUNKS, LANES), F32),
                        pltpu.SemaphoreType.DMA, pltpu.SemaphoreType.DMA((2,))],
        compiler_params=pltpu.CompilerParams(
            dimension_semantics=("arbitrary",), vmem_limit_bytes=VMEM_LIMIT_BYTES),
        name="moe_combine",
    )(h3, route, dest, dest, yb)


def _moe_layer(h3, hn2, route, wg, wu, wd):
    t, d = h3.shape
    n_tiles = (t * TOP_K) // MOE_TILE + N_EXPERTS
    m_pad = n_tiles * MOE_TILE
    experts = route[:, :TOP_K].astype(jnp.int32).T
    rank, counts = _route_ranks(experts)
    tiles_per = (counts + MOE_TILE - 1) // MOE_TILE
    tile_end = jnp.cumsum(tiles_per)
    start = ((tile_end - tiles_per) * MOE_TILE).astype(jnp.int32)
    tile_ids = jnp.arange(n_tiles, dtype=jnp.int32)
    tile_valid = (tile_ids < tile_end[-1]).astype(jnp.int32)
    last_valid = jnp.maximum(tile_end[-1] - 1, 0)
    tile_expert = jnp.minimum(
        jnp.sum(jnp.minimum(tile_ids, last_valid)[:, None] >= tile_end[None, :], axis=1),
        N_EXPERTS - 1).astype(jnp.int32)
    has_pad = (counts % MOE_TILE) != 0
    pad_tile = jnp.where(has_pad, tile_end - 1, n_tiles)
    zmask = jnp.zeros((n_tiles + 1,), jnp.int32).at[pad_tile].set(1)[:n_tiles]
    zmask = jnp.maximum(zmask, 1 - tile_valid)
    nz = jnp.sum(zmask).astype(jnp.int32).reshape(1)
    zorder = jnp.argsort(1 - zmask, stable=True).astype(jnp.int32)
    zrows = (zorder[:2 * N_EXPERTS] * MOE_TILE).astype(jnp.int32)
    xs, dest = _dispatch(hn2, experts, rank, start, zrows, nz, m_pad)
    yb = _experts(xs, tile_expert, tile_valid, wg, wu, wd)
    return _combine(h3, route, dest, yb)


def _stages(x, hgrn_w_in, hgrn_lb, hgrn_gnorm, hgrn_w_out, swa_w_q, swa_q_gain, swa_sink, swa_w_o, kv_norm, kv_w, k_gain, rel_bias, attn_norm, ffn_norm, ffn_w_gate, ffn_w_up, ffn_w_down, moe_router, moe_w_gate, moe_w_up, moe_w_down):
    bsz, seq, d = x.shape
    t = bsz * seq
    bf = lambda w: w.astype(BF16)
    h1 = _hgrn_layer(x, attn_norm[0], bf(hgrn_w_in[0]), hgrn_lb, hgrn_gnorm[0], bf(hgrn_w_out[0]))
    h2, q, k2, v2 = _ffn_kvq(h1.reshape(t, d), ffn_norm[0], bf(ffn_w_gate[0]), bf(ffn_w_up[0]),
                             bf(ffn_w_down[0]), kv_norm, kv_w, k_gain, attn_norm[1], bf(swa_w_q[0]))
    nk = SWA_KV_HEADS * SWA_KV_LANES
    h3, hn2, route = _attn_router(q.reshape(bsz, seq, d), k2.reshape(bsz, seq, nk),
                                  v2.reshape(bsz, seq, nk), h2.reshape(bsz, seq, d), rel_bias,
                                  swa_sink[0], swa_q_gain[0], bf(swa_w_o[0]), ffn_norm[1],
                                  moe_router[0])
    h4 = _moe_layer(h3.reshape(t, d), hn2, route.reshape(t, ROUTE_COLS),
                    bf(moe_w_gate[0]), bf(moe_w_up[0]), bf(moe_w_down[0]))
    return {"h1": h1, "h2": h2.reshape(bsz, seq, d), "h3": h3, "h4": h4.reshape(bsz, seq, d)}


def kernel(x, hgrn_w_in, hgrn_lb, hgrn_gnorm, hgrn_w_out, swa_w_q, swa_q_gain, swa_sink, swa_w_o, kv_norm, kv_w, k_gain, rel_bias, attn_norm, ffn_norm, ffn_w_gate, ffn_w_up, ffn_w_down, moe_router, moe_w_gate, moe_w_up, moe_w_down):
    return _stages(x, hgrn_w_in, hgrn_lb, hgrn_gnorm, hgrn_w_out, swa_w_q, swa_q_gain, swa_sink, swa_w_o, kv_norm, kv_w, k_gain, rel_bias, attn_norm, ffn_norm, ffn_w_gate, ffn_w_up, ffn_w_down, moe_router, moe_w_gate, moe_w_up, moe_w_down)["h4"]
```

```python
import functools

import numpy as np
import jax
import jax.numpy as jnp
from jax import lax
from jax.experimental import pallas as pl
from jax.experimental.pallas import tpu as pltpu

F32 = jnp.float32
BF16 = jnp.bfloat16

D_MODEL = 1024
NORM_EPS = 1e-6
LOG2_E = 1.4426950408889634

LANES = 128
MXU_WIDTH = 256

HG_HEADS = 8
HG_DK = 128
HG_DV = 128
HG_CHUNK = 128
HG_LEVELS = 7
HG_STEP_CHUNKS = 4
HG_GROUP_HEADS = 8

VMEM_LIMIT_BYTES = 56 * 1024 * 1024


def _dot(a, b):
    return jnp.dot(a, b, preferred_element_type=F32)


def _dot_nt(a, b):
    return lax.dot_general(a, b, (((1,), (1,)), ((), ())), preferred_element_type=F32)


def _dot_tn(a, b):
    return lax.dot_general(a, b, (((0,), (0,)), ((), ())), preferred_element_type=F32)


def _rms_scale(x):
    return lax.rsqrt(jnp.mean(x * x, axis=-1, keepdims=True) + NORM_EPS)


def _sigmoid(x):
    return 1.0 / (1.0 + jnp.exp(-x))


ROW_CHUNKS = D_MODEL // LANES


def _load_token_rows(ref, n, lead=()):
    return jnp.concatenate(
        [ref[lead + (pl.ds(c, n, stride=ROW_CHUNKS), slice(None))] for c in range(ROW_CHUNKS)], axis=1)


def _store_token_rows(ref, x):
    n = x.shape[0]
    for c in range(ROW_CHUNKS):
        ref[pl.ds(c, n, stride=ROW_CHUNKS), :] = x[:, c * LANES:(c + 1) * LANES]


def _token_row_slice(ref, token):
    return ref.at[pl.ds(pl.multiple_of(token * ROW_CHUNKS, ROW_CHUNKS), ROW_CHUNKS)]


def _const_spec(shape):
    nd = len(shape)
    return pl.BlockSpec(shape, lambda *_: (0,) * nd, pipeline_mode=pl.Buffered(1))


def _hgrn_level_log_decay(b, level):
    c = HG_CHUNK
    m = 1 << level
    parts = []
    if 2 * m >= 8:
        for start in range(0, c, 2 * m):
            mid = start + m - 1
            parts.append(b[start:start + 2 * m] - b[mid:mid + 1])
    else:
        first_block = lax.broadcasted_iota(jnp.int32, (8, b.shape[1]), 0) < 4
        for start in range(0, c, 8):
            b_mid = jnp.where(first_block, b[start + 1:start + 2], b[start + 5:start + 6])
            parts.append(b[start:start + 8] - b_mid)
    return jnp.concatenate(parts, axis=0)


def _hgrn_kernel(x_ref, gain_ref, win_ref, lbp_ref, tsum_ref, gnorm_ref, wout_ref,
                 out_ref, st_ref, o_ref):
    c = HG_CHUNK

    @pl.when(pl.program_id(1) == 0)
    def _():
        st_ref[...] = jnp.zeros_like(st_ref)

    chunks = [slice(ci * c, (ci + 1) * c) for ci in range(HG_STEP_CHUNKS)]

    lbp = lbp_ref[...]
    lbe = jnp.exp(lbp - jnp.max(lbp, axis=0, keepdims=True))
    lb = lbe[0:1] / jnp.sum(lbe, axis=0, keepdims=True)

    x = x_ref[0]
    hn = (x * _rms_scale(x) * gain_ref[...]).astype(BF16)

    ti = lax.broadcasted_iota(jnp.int32, (c, c), 0)
    si = lax.broadcasted_iota(jnp.int32, (c, c), 1)
    diag = ti == si
    xor = ti ^ si
    lv_mask = [((xor >> l) == 1) & (((ti >> l) & 1) == 1) for l in range(HG_LEVELS)]
    keep = [jnp.where(m, 1.0, 0.0) for m in [diag] + lv_mask]

    row = lax.broadcasted_iota(jnp.int32, (c, HG_DK), 0)
    upper_rows = [((row >> l) & 1) == 1 for l in range(HG_LEVELS)]
    signs = [jnp.where(u, LOG2_E, -LOG2_E) for u in upper_rows]

    gw = HG_GROUP_HEADS * HG_DK
    for p in range(HG_HEADS // HG_GROUP_HEADS):
        cs = slice(p * gw, (p + 1) * gw)
        sec = [_dot(hn, win_ref[:, i * D_MODEL + p * gw:i * D_MODEL + (p + 1) * gw]) for i in range(4)]
        f = lb[:, cs] + (1.0 - lb[:, cs]) * _sigmoid(sec[1])
        g = jnp.log(f)
        g_hi = g.astype(BF16)
        g_lo = (g - g_hi.astype(F32)).astype(BF16)
        b = [_dot(tsum_ref[...], jnp.concatenate([g_hi[rs], g_lo[rs]], axis=0)) for rs in chunks]

        staged = []
        for ci, rs in enumerate(chunks):
            for hh in range(HG_GROUP_HEADS):
                ls = slice(hh * HG_DK, (hh + 1) * HG_DK)
                q_h = sec[0][rs, ls]
                f_h = f[rs, ls]
                b_h = b[ci][:, ls]
                k_h = 1.0 - f_h
                b_last = b_h[c - 1:c]
                per_level = []
                for l in range(HG_LEVELS):
                    if l == 0:
                        w = jnp.where(upper_rows[0], f_h, 1.0)
                    else:
                        w = jnp.exp2(_hgrn_level_log_decay(b_h, l) * signs[l])
                    per_level.append((jnp.where(upper_rows[l], q_h, k_h) * w).astype(BF16))
                staged.append((rs, p * HG_GROUP_HEADS + hh, dict(
                    levels=per_level, q=q_h.astype(BF16), k=k_h.astype(BF16),
                    qb=(q_h * jnp.exp(b_h)).astype(BF16),
                    khat=(k_h * jnp.exp(b_last - b_h)).astype(BF16),
                    decay=jnp.exp(b_last),
                    v=sec[2][rs, ls].astype(BF16),
                    o_gate=sec[3][rs, ls])))

        scores = []
        for _, _, s in staged:
            a = _dot_nt(s["q"], s["k"]) * keep[0]
            for m, lv in zip(keep[1:], s["levels"]):
                a = a + _dot_nt(lv, lv) * m
            scores.append(a.astype(BF16))

        outs = []
        for (_, h, s), a in zip(staged, scores):
            st = st_ref[h]
            outs.append(_dot(a, s["v"]) + _dot_nt(s["qb"], st.astype(BF16)))
            st_ref[h] = st * s["decay"] + _dot_tn(s["v"], s["khat"])

        for (rs, h, s), o_h in zip(staged, outs):
            o_h = o_h * _rms_scale(o_h) * gnorm_ref[...]
            og = s["o_gate"]
            o_ref[rs, h * HG_DV:(h + 1) * HG_DV] = (o_h * (og * _sigmoid(og))).astype(BF16)

    out_ref[0] = x + _dot(o_ref[...], wout_ref[...])


def _hgrn_layer(x, gain, w_in, lb_param, gnorm, w_out):
    bsz, seq, d = x.shape
    c = HG_CHUNK * HG_STEP_CHUNKS
    tril = np.tril(np.ones((HG_CHUNK, HG_CHUNK), np.float32))
    tsum2 = jnp.asarray(np.concatenate([tril, tril], axis=1), dtype=BF16)
    return pl.pallas_call(
        _hgrn_kernel,
        out_shape=jax.ShapeDtypeStruct((bsz, seq, d), F32),
        grid=(bsz, seq // c),
        in_specs=[
            pl.BlockSpec((1, c, d), lambda b, j: (b, j, 0)),
            _const_spec((1, d)),
            _const_spec((d, 4 * d)),
            _const_spec(lb_param.shape),
            _const_spec(tsum2.shape),
            _const_spec((1, HG_DV)),
            _const_spec((d, d)),
        ],
        out_specs=pl.BlockSpec((1, c, d), lambda b, j: (b, j, 0)),
        scratch_shapes=[
            pltpu.VMEM((HG_HEADS, HG_DV, HG_DK), F32),
            pltpu.VMEM((c, d), BF16),
        ],
        compiler_params=pltpu.CompilerParams(
            dimension_semantics=("arbitrary", "arbitrary"),
            vmem_limit_bytes=VMEM_LIMIT_BYTES),
        name="hgrn_layer",
    )(x, gain.reshape(1, d), w_in, lb_param, tsum2, gnorm.reshape(1, HG_DV), w_out)


D_FF_DENSE = 2816
FFN_FF_SPLITS = tuple(MXU_WIDTH * n for n in (0, 3, 6, 9, 11))
FFN_ROWS = 512

SWA_HEAD_DIM = 64
SWA_Q_HEADS = 16
SWA_KV_HEADS = 4
SWA_WINDOW = 128
SWA_KV_LANES = 2 * SWA_HEAD_DIM


def _ffn_kvq_kernel(h_ref, fgain_ref, wg_ref, wu_ref, wd_ref, kvgain_ref, kvw_ref, kgain_ref,
                    qgain_ref, wq_ref, h2_ref, q_ref, k_ref, v_ref):
    h = h_ref[...]
    hn = (h * _rms_scale(h) * fgain_ref[...]).astype(BF16)
    acc = h
    for c in range(len(FFN_FF_SPLITS) - 1):
        sl = slice(FFN_FF_SPLITS[c], FFN_FF_SPLITS[c + 1])
        gate = _dot(hn, wg_ref[:, sl])
        up = _dot(hn, wu_ref[:, sl])
        hid = (gate * _sigmoid(gate) * up).astype(BF16)
        acc = acc + _dot(hid, wd_ref[sl, :])
    h2_ref[...] = acc
    normed = acc * _rms_scale(acc)
    kv = _dot((normed * kvgain_ref[...]).astype(BF16), kvw_ref[...])
    nk = SWA_KV_HEADS * SWA_KV_LANES
    for g in range(SWA_KV_HEADS):
        sl = slice(g * SWA_KV_LANES, (g + 1) * SWA_KV_LANES)
        kg = kv[:, sl]
        k_ref[:, sl] = (kg * _rms_scale(kg) * kgain_ref[...]).astype(BF16)
    v_ref[...] = kv[:, nk:].astype(BF16)
    q_ref[...] = _dot((normed * qgain_ref[...]).astype(BF16), wq_ref[...]).astype(BF16)


def _dup_kv_columns(w):
    d = w.shape[0]
    w = w.reshape(d, SWA_KV_HEADS, 1, SWA_HEAD_DIM)
    return jnp.broadcast_to(w, (d, SWA_KV_HEADS, 2, SWA_HEAD_DIM)).reshape(d, SWA_KV_HEADS * SWA_KV_LANES)


def _ffn_kvq(h1, fgain, wg, wu, wd, kvgain, kv_w, k_gain, qgain, wq):
    t, d = h1.shape
    rows = min(FFN_ROWS, t)
    nkv = SWA_KV_HEADS * SWA_HEAD_DIM
    kvw2 = jnp.concatenate([_dup_kv_columns(kv_w[:, :nkv]), _dup_kv_columns(kv_w[:, nkv:])],
                           axis=1).astype(BF16)
    kgain2 = jnp.concatenate([k_gain, k_gain]).reshape(1, SWA_KV_LANES)
    nk = SWA_KV_HEADS * SWA_KV_LANES
    row_spec = lambda w: pl.BlockSpec((rows, w), lambda i: (i, 0))
    return pl.pallas_call(
        _ffn_kvq_kernel,
        out_shape=(jax.ShapeDtypeStruct((t, d), F32), jax.ShapeDtypeStruct((t, d), BF16),
                   jax.ShapeDtypeStruct((t, nk), BF16), jax.ShapeDtypeStruct((t, nk), BF16)),
        grid=(t // rows,),
        in_specs=[
            row_spec(d),
            _const_spec((1, d)),
            _const_spec((d, D_FF_DENSE)), _const_spec((d, D_FF_DENSE)), _const_spec((D_FF_DENSE, d)),
            _const_spec((1, d)), _const_spec((d, 2 * nk)), _const_spec((1, SWA_KV_LANES)),
            _const_spec((1, d)), _const_spec((d, d)),
        ],
        out_specs=(row_spec(d), row_spec(d), row_spec(nk), row_spec(nk)),
        compiler_params=pltpu.CompilerParams(
            dimension_semantics=("arbitrary",), vmem_limit_bytes=VMEM_LIMIT_BYTES),
        name="ffn_kvq",
    )(h1, fgain.reshape(1, d), wg, wu, wd, kvgain.reshape(1, d), kvw2, kgain2,
      qgain.reshape(1, d), wq)


ATTN_ROWS = 512
REL_BUCKETS = 32
REL_MAX_DIST = 128
N_EXPERTS = 8
ROUTE_LANES = 128
ROUTE_COLS = 8


def _rel_bucket_table():
    w = SWA_WINDOW
    qi = np.arange(w)[:, None]
    kj = np.arange(2 * w)[None, :]
    dist = qi + w - kj
    in_win = (dist >= 0) & (dist < w)
    n = np.maximum(dist, 0)
    max_exact = REL_BUCKETS // 2
    nf = np.maximum(n, 1).astype(np.float64)
    large = max_exact + (np.log(nf / max_exact) / np.log(REL_MAX_DIST / max_exact)
                         * (REL_BUCKETS - max_exact)).astype(np.int64)
    large = np.minimum(large, REL_BUCKETS - 1)
    bucket = np.where(n < max_exact, n, large)
    return np.where(in_win, bucket, -1).astype(np.int32)


def _attn_router_kernel(tiles_per_seq, relb_ref, sink_ref, q_ref, kc_ref, kp_ref, vc_ref, vp_ref,
                        h2_ref, bucket_ref, qgain_ref, wo_ref, fgain_ref, wr_ref,
                        h3_ref, hn2_ref, route_ref, bias_ref, attn_ref, hnprev_ref):
    w = SWA_WINDOW
    hd = SWA_HEAD_DIM
    rows = q_ref.shape[1]
    step = pl.program_id(0)
    last_tile = pl.num_programs(0) - 2
    first_tile = jnp.minimum(step, last_tile) % tiles_per_seq == 0

    @pl.when(step == 0)
    def _():
        hnprev_ref[...] = jnp.zeros_like(hnprev_ref)
        bucket = bucket_ref[...]
        for h in range(SWA_Q_HEADS):
            plane = jnp.full((w, 2 * w), -jnp.inf, F32)
            for bkt in range(REL_BUCKETS):
                plane = jnp.where(bucket == bkt, relb_ref[bkt * SWA_Q_HEADS + h], plane)
            bias_ref[h] = plane

    hp = hnprev_ref[...]
    hi = hp.astype(BF16)
    lo = (hp - hi.astype(F32)).astype(BF16)
    logits = _dot(hi, wr_ref[0]) + _dot(lo, wr_ref[0]) + _dot(hi, wr_ref[1])
    col = lax.broadcasted_iota(jnp.int32, logits.shape, 1)
    logits = jnp.where(col < N_EXPERTS, logits, -jnp.inf)
    m0 = jnp.max(logits, axis=-1, keepdims=True)
    i0 = jnp.min(jnp.where(logits == m0, col, ROUTE_LANES), axis=-1, keepdims=True)
    rest = jnp.where(col == i0, -jnp.inf, logits)
    m1 = jnp.max(rest, axis=-1, keepdims=True)
    i1 = jnp.min(jnp.where(rest == m1, col, ROUTE_LANES), axis=-1, keepdims=True)
    e1 = jnp.exp(m1 - m0)
    g0 = 1.0 / (1.0 + e1)
    g1 = e1 / (1.0 + e1)
    rc = lax.broadcasted_iota(jnp.int32, (rows, ROUTE_COLS), 1)
    route_ref[0] = jnp.where(rc == 0, i0.astype(F32),
                             jnp.where(rc == 1, i1.astype(F32),
                                       jnp.where(rc == 2, g0, jnp.where(rc == 3, g1, 0.0))))


    lane = lax.broadcasted_iota(jnp.int32, (2 * w, SWA_KV_LANES), 1)
    left = lane < hd
    lane_q = lax.broadcasted_iota(jnp.int32, (w, SWA_KV_LANES), 1)
    left_q = lane_q < hd
    key_idx = lax.broadcasted_iota(jnp.int32, (1, 2 * w), 1)
    qgain = qgain_ref[...]

    k_all = jnp.concatenate([kp_ref[0], kc_ref[0]], axis=0)
    v_all = jnp.concatenate([vp_ref[0], vc_ref[0]], axis=0)
    zero = jnp.zeros((), BF16)
    no_prev = jnp.where(jnp.logical_and(first_tile, key_idx < w), -jnp.inf, 0.0)
    units = [(blk, g) for blk in range(rows // w) for g in range(SWA_KV_HEADS)]

    score, values = [], []
    for blk, g in units:
        r0 = blk * w
        ks = slice(g * SWA_KV_LANES, (g + 1) * SWA_KV_LANES)
        k2 = k_all[r0:r0 + 2 * w, ks]
        v2 = v_all[r0:r0 + 2 * w, ks]
        k_bd = jnp.concatenate([jnp.where(left, k2, zero), jnp.where(left, zero, k2)], axis=0)
        values.append(jnp.concatenate([jnp.where(left, v2, zero), jnp.where(left, zero, v2)], axis=0))
        pairs = []
        for p in range(2):
            c0 = (4 * g + 2 * p) * hd
            qp = q_ref[0, r0:r0 + w, c0:c0 + 2 * hd].astype(F32)
            sq = qp * qp
            s_l = jnp.sum(jnp.where(left_q, sq, 0.0), axis=-1, keepdims=True)
            s_r = jnp.sum(jnp.where(left_q, 0.0, sq), axis=-1, keepdims=True)
            ms = jnp.where(left_q, s_l, s_r) * (1.0 / hd)
            pairs.append((qp * lax.rsqrt(ms + NORM_EPS) * qgain * (hd ** -0.5)).astype(BF16))
        qs = jnp.concatenate(pairs, axis=0)
        score.append(_dot_nt(qs, k_bd))

    probs, sinks = [], []
    for (blk, g), s in zip(units, score):
        e_rows, sink_rows = [], []
        for p in range(2):
            halves = []
            for side in range(2):
                hq = 4 * g + 2 * p + side
                logit = s[p * w:(p + 1) * w, side * 2 * w:(side + 1) * 2 * w] + bias_ref[hq]
                if blk == 0:
                    logit = logit + no_prev
                sink = sink_ref[hq]
                m = jnp.maximum(jnp.max(logit, axis=-1, keepdims=True), sink)
                halves.append((jnp.exp(logit - m), jnp.exp(sink - m)))
            e_rows.append(jnp.concatenate([halves[0][0], halves[1][0]], axis=1).astype(BF16))
            sink_rows.append(jnp.where(left_q, halves[0][1], halves[1][1]))
        probs.append(jnp.concatenate(e_rows, axis=0))
        sinks.append(jnp.concatenate(sink_rows, axis=0))

    ones_bd = jnp.concatenate([jnp.where(left, 1.0, 0.0), jnp.where(left, 0.0, 1.0)], axis=0).astype(BF16)
    for (blk, g), e2, v_bd, sink_term in zip(units, probs, values, sinks):
        pvd = _dot(e2, jnp.concatenate([v_bd, ones_bd], axis=1))
        pv = pvd[:, :SWA_KV_LANES] / (pvd[:, SWA_KV_LANES:] + sink_term)
        r0 = blk * w
        for p in range(2):
            c0 = (4 * g + 2 * p) * hd
            attn_ref[r0:r0 + w, c0:c0 + 2 * hd] = pv[p * w:(p + 1) * w].astype(BF16)

    h3 = h2_ref[0] + _dot(attn_ref[...], wo_ref[...])
    h3_ref[0] = h3
    hn2 = h3 * _rms_scale(h3) * fgain_ref[...]
    _store_token_rows(hn2_ref, hn2)
    hnprev_ref[...] = hn2


def _attn_router(q, k2, v2, h2, rel_bias, sink, q_gain, w_o, fgain, w_router):
    bsz, seq, d = h2.shape
    rows = min(ATTN_ROWS, seq)
    w = SWA_WINDOW
    nk = SWA_KV_HEADS * SWA_KV_LANES
    per = rows // w
    nj = seq // rows
    n_tiles = bsz * nj
    bucket = jnp.asarray(_rel_bucket_table())
    qgain2 = jnp.concatenate([q_gain, q_gain]).reshape(1, SWA_KV_LANES)
    wr = jnp.zeros((d, ROUTE_LANES), F32).at[:, :N_EXPERTS].set(w_router)
    wr_hi = wr.astype(BF16)
    wr_lo = (wr - wr_hi.astype(F32)).astype(BF16)
    wr2 = jnp.stack([wr_hi, wr_lo])
    cur = lambda s: jnp.minimum(s, n_tiles - 1)
    tile = lambda width: pl.BlockSpec((1, rows, width), lambda s, *_: (cur(s) // nj, cur(s) % nj, 0))
    prev = pl.BlockSpec(
        (1, w, nk), lambda s, *_: (cur(s) // nj, jnp.maximum((cur(s) % nj) * per - 1, 0), 0))
    routed = lambda s: jnp.maximum(s - 1, 0)
    const = lambda shape: pl.BlockSpec(shape, lambda s, *_: (0,) * len(shape),
                                       pipeline_mode=pl.Buffered(1))
    return pl.pallas_call(
        functools.partial(_attn_router_kernel, nj),
        out_shape=(jax.ShapeDtypeStruct((bsz, seq, d), F32),
                   jax.ShapeDtypeStruct((bsz * seq * ROW_CHUNKS, LANES), F32),
                   jax.ShapeDtypeStruct((bsz, seq, ROUTE_COLS), F32)),
        grid_spec=pltpu.PrefetchScalarGridSpec(
            num_scalar_prefetch=2,
            grid=(n_tiles + 1,),
            in_specs=[tile(d), tile(nk), prev, tile(nk), prev, tile(d),
                      const((w, 2 * w)), const((1, SWA_KV_LANES)), const((d, d)), const((1, d)),
                      const((2, d, ROUTE_LANES))],
            out_specs=(tile(d),
                       pl.BlockSpec((rows * ROW_CHUNKS, LANES), lambda s, *_: (cur(s), 0)),
                       pl.BlockSpec((1, rows, ROUTE_COLS),
                                    lambda s, *_: (routed(s) // nj, routed(s) % nj, 0))),
            scratch_shapes=[pltpu.VMEM((SWA_Q_HEADS, w, 2 * w), F32), pltpu.VMEM((rows, d), BF16),
                            pltpu.VMEM((rows, d), F32)],
        ),
        compiler_params=pltpu.CompilerParams(
            dimension_semantics=("arbitrary",), vmem_limit_bytes=VMEM_LIMIT_BYTES),
        name="attn_router",
    )(rel_bias.reshape(-1), sink, q, k2, k2, v2, v2, h2, bucket, qgain2, w_o, fgain.reshape(1, d), wr2)


TOP_K = 2
D_FF_EXPERT = 3584
MOE_TILE = 512
MOE_FF_SUB = 512
RANK_BLOCK = 1024
DISPATCH_BLOCK = 2048
COMBINE_BLOCK = 1024
ZERO_ROWS = 256
RANK_SUBLANES = 16
DMA_PRIORITIES = 2


def _rank_kernel(e_ref, tri_ref, rank_ref, cnt_ref, carry_ref):
    bt = e_ref.shape[1]

    @pl.when(pl.program_id(0) == 0)
    def _():
        carry_ref[...] = jnp.zeros_like(carry_ref)

    ex = lax.broadcasted_iota(jnp.int32, (RANK_SUBLANES, bt), 0)
    oh0 = ex == e_ref[0:1, :]
    oh1 = ex == e_ref[1:2, :]
    both = oh0.astype(F32) + oh1.astype(F32)
    pos = _dot(both.astype(BF16), tri_ref[...]) + carry_ref[:, 0:1]
    rank_ref[0:1, :] = jnp.sum(jnp.where(oh0, pos, 0.0), axis=0, keepdims=True).astype(jnp.int32)
    rank_ref[1:2, :] = jnp.sum(jnp.where(oh1, pos, 0.0), axis=0, keepdims=True).astype(jnp.int32)
    carry_ref[...] = carry_ref[...] + jnp.sum(both, axis=1, keepdims=True)
    cnt_ref[...] = carry_ref[...]


def _route_ranks(experts):
    _, t = experts.shape
    bt = min(RANK_BLOCK, t)
    tri = jnp.asarray(np.triu(np.ones((bt, bt), np.float32), k=1), dtype=BF16)
    rank, cnt = pl.pallas_call(
        _rank_kernel,
        out_shape=(jax.ShapeDtypeStruct((TOP_K, t), jnp.int32),
                   jax.ShapeDtypeStruct((RANK_SUBLANES, LANES), F32)),
        grid=(t // bt,),
        in_specs=[pl.BlockSpec((TOP_K, bt), lambda i: (0, i)), _const_spec((bt, bt))],
        out_specs=(pl.BlockSpec((TOP_K, bt), lambda i: (0, i)),
                   pl.BlockSpec((RANK_SUBLANES, LANES), lambda i: (0, 0))),
        scratch_shapes=[pltpu.VMEM((RANK_SUBLANES, LANES), F32)],
        compiler_params=pltpu.CompilerParams(dimension_semantics=("arbitrary",)),
        name="route_rank",
    )(experts, tri)
    return rank, cnt[:N_EXPERTS, 0].astype(jnp.int32)


def _row_copy(src, src_token, dst, dst_token, sem):
    return pltpu.make_async_copy(_token_row_slice(src, src_token), _token_row_slice(dst, dst_token), sem)


def _dispatch_kernel(start_ref, zrow_ref, nz_ref, hn_ref, e_ref, rank_ref, xs_ref, dest_ref,
                     dvm_ref, dsm_ref, zero_ref, sem_idx, sem_rows, sem_zero):
    bt = e_ref.shape[1]
    zero_copies = MOE_TILE // ZERO_ROWS

    def zero_copy(z, c):
        row = pl.multiple_of((zrow_ref[z] + c * ZERO_ROWS) * ROW_CHUNKS, ROW_CHUNKS)
        return pltpu.make_async_copy(zero_ref, xs_ref.at[pl.ds(row, ZERO_ROWS * ROW_CHUNKS)], sem_zero)

    @pl.when(pl.program_id(0) == 0)
    def _():
        zero_ref[...] = jnp.zeros_like(zero_ref)

        def start(z, carry):
            for c in range(zero_copies):
                zero_copy(z, c).start()
            return carry

        def wait(z, carry):
            for c in range(zero_copies):
                zero_copy(z, c).wait()
            return carry

        lax.fori_loop(0, nz_ref[0], start, 0)
        lax.fori_loop(0, nz_ref[0], wait, 0)

    e = e_ref[...]
    base = jnp.zeros_like(e)
    for x in range(N_EXPERTS):
        base = jnp.where(e == x, start_ref[x], base)
    dest = base + rank_ref[...]
    dest_ref[...] = dest
    dvm_ref[...] = dest
    idx_copy = pltpu.make_async_copy(dvm_ref, dsm_ref, sem_idx)
    idx_copy.start()
    idx_copy.wait()

    def issue(t, carry):
        for c in range(TOP_K):
            _row_copy(hn_ref, t, xs_ref, dsm_ref[c, t], sem_rows).start(priority=c % DMA_PRIORITIES)
        return carry

    lax.fori_loop(0, bt, issue, 0, unroll=8)
    for c in range(TOP_K):
        pltpu.make_async_copy(hn_ref, xs_ref.at[pl.ds(0, bt * ROW_CHUNKS)], sem_rows).wait()


def _dispatch(hn, experts, rank, start, zrows, nz, m_pad):
    t = hn.shape[0] // ROW_CHUNKS
    bt = min(DISPATCH_BLOCK, t)
    blk = lambda i, *_: (0, i)
    return pl.pallas_call(
        _dispatch_kernel,
        out_shape=(jax.ShapeDtypeStruct((m_pad * ROW_CHUNKS, LANES), F32),
                   jax.ShapeDtypeStruct((TOP_K, t), jnp.int32)),
        grid_spec=pltpu.PrefetchScalarGridSpec(
            num_scalar_prefetch=3,
            grid=(t // bt,),
            in_specs=[pl.BlockSpec((bt * ROW_CHUNKS, LANES), lambda i, *_: (i, 0)),
                      pl.BlockSpec((TOP_K, bt), blk), pl.BlockSpec((TOP_K, bt), blk)],
            out_specs=(pl.BlockSpec(memory_space=pl.ANY), pl.BlockSpec((TOP_K, bt), blk)),
            scratch_shapes=[pltpu.VMEM((TOP_K, bt), jnp.int32), pltpu.SMEM((TOP_K, bt), jnp.int32),
                            pltpu.VMEM((ZERO_ROWS * ROW_CHUNKS, LANES), F32),
                            pltpu.SemaphoreType.DMA, pltpu.SemaphoreType.DMA, pltpu.SemaphoreType.DMA],
        ),
        compiler_params=pltpu.CompilerParams(dimension_semantics=("arbitrary",)),
        name="moe_dispatch",
    )(start, zrows, nz, hn, experts, rank)


def _expert_kernel(te_ref, tv_ref, x_ref, wg_ref, wu_ref, wd_ref, out_ref):
    valid = tv_ref[pl.program_id(0)] == 1

    @pl.when(valid)
    def _():
        xb = _load_token_rows(x_ref, MOE_TILE).astype(BF16)
        acc = jnp.zeros((MOE_TILE, D_MODEL), F32)
        for j in range(D_FF_EXPERT // MOE_FF_SUB):
            sl = slice(j * MOE_FF_SUB, (j + 1) * MOE_FF_SUB)
            gate = _dot(xb, wg_ref[0, :, sl])
            up = _dot(xb, wu_ref[0, :, sl])
            hid = (gate * _sigmoid(gate) * up).astype(BF16)
            acc = acc + _dot(hid, wd_ref[0, sl, :])
        _store_token_rows(out_ref, acc)

    @pl.when(jnp.logical_not(valid))
    def _():
        out_ref[...] = jnp.zeros_like(out_ref)


def _experts(xs, tile_expert, tile_valid, wg, wu, wd):
    d = D_MODEL
    row_block = pl.BlockSpec((MOE_TILE * ROW_CHUNKS, LANES), lambda i, te, tv: (i, 0))
    weights = lambda shape, bufs: pl.BlockSpec((1,) + shape, lambda i, te, tv: (te[i], 0, 0),
                                               pipeline_mode=pl.Buffered(bufs))
    return pl.pallas_call(
        _expert_kernel,
        out_shape=jax.ShapeDtypeStruct(xs.shape, F32),
        grid_spec=pltpu.PrefetchScalarGridSpec(
            num_scalar_prefetch=2,
            grid=(xs.shape[0] // (MOE_TILE * ROW_CHUNKS),),
            in_specs=[row_block, weights((d, D_FF_EXPERT), 2), weights((d, D_FF_EXPERT), 1),
                      weights((D_FF_EXPERT, d), 2)],
            out_specs=row_block,
        ),
        compiler_params=pltpu.CompilerParams(
            dimension_semantics=("arbitrary",), vmem_limit_bytes=VMEM_LIMIT_BYTES),
        name="moe_experts",
    )(tile_expert, tile_valid, xs, wg, wu, wd)


def _combine_kernel(h_ref, route_ref, dest_ref, dest_next_ref, yb_ref, out_ref,
                    dsm_ref, ybuf_ref, sem_idx, sem_rows):
    bt = h_ref.shape[0]
    i = pl.program_id(0)
    slot = i % 2

    def start_gathers(step_dest_ref, s):
        idx_copy = pltpu.make_async_copy(step_dest_ref, dsm_ref.at[s], sem_idx)
        idx_copy.start()
        idx_copy.wait()

        def issue(t, carry):
            for c in range(TOP_K):
                _row_copy(yb_ref, dsm_ref[s, c, t], ybuf_ref.at[s, c], t, sem_rows.at[s]).start(
                    priority=c % DMA_PRIORITIES)
            return carry

        lax.fori_loop(0, bt, issue, 0, unroll=8)

    @pl.when(i == 0)
    def _():
        start_gathers(dest_ref, 0)

    @pl.when(i + 1 < pl.num_programs(0))
    def _():
        start_gathers(dest_next_ref, 1 - slot)

    for c in range(TOP_K):
        pltpu.make_async_copy(yb_ref.at[pl.ds(0, bt * ROW_CHUNKS)], ybuf_ref.at[slot, c],
                              sem_rows.at[slot]).wait()
    route = route_ref[...]
    y0 = _load_token_rows(ybuf_ref, bt, (slot, 0))
    y1 = _load_token_rows(ybuf_ref, bt, (slot, 1))
    out_ref[...] = h_ref[...] + route[:, 2:3] * y0 + route[:, 3:4] * y1


def _combine(h3, route, dest, yb):
    t, d = h3.shape
    bt = min(COMBINE_BLOCK, t)
    n = t // bt
    return pl.pallas_call(
        _combine_kernel,
        out_shape=jax.ShapeDtypeStruct((t, d), F32),
        grid=(n,),
        in_specs=[pl.BlockSpec((bt, d), lambda i: (i, 0)),
                  pl.BlockSpec((bt, ROUTE_COLS), lambda i: (i, 0)),
                  pl.BlockSpec((TOP_K, bt), lambda i: (0, i)),
                  pl.BlockSpec((TOP_K, bt), lambda i: (0, jnp.minimum(i + 1, n - 1))),
                  pl.BlockSpec(memory_space=pl.ANY)],
        out_specs=pl.BlockSpec((bt, d), lambda i: (i, 0)),
        scratch_shapes=[pltpu.SMEM((2, TOP_K, bt), jnp.int32),
                        pltpu.VMEM((2, TOP_K, bt * ROW_CHUNKS, LANES), F32),
                        pltpu.SemaphoreType.DMA, pltpu.SemaphoreType.DMA((2,))],
        compiler_params=pltpu.CompilerParams(
            dimension_semantics=("arbitrary",), vmem_limit_bytes=VMEM_LIMIT_BYTES),
        name="moe_combine",
    )(h3, route, dest, dest, yb)


def _moe_layer(h3, hn2, route, wg, wu, wd):
    t, d = h3.shape
    n_tiles = (t * TOP_K) // MOE_TILE + N_EXPERTS
    m_pad = n_tiles * MOE_TILE
    experts = route[:, :TOP_K].astype(jnp.int32).T
    rank, counts = _route_ranks(experts)
    tiles_per = (counts + MOE_TILE - 1) // MOE_TILE
    tile_end = jnp.cumsum(tiles_per)
    start = ((tile_end - tiles_per) * MOE_TILE).astype(jnp.int32)
    tile_ids = jnp.arange(n_tiles, dtype=jnp.int32)
    tile_valid = (tile_ids < tile_end[-1]).astype(jnp.int32)
    last_valid = jnp.maximum(tile_end[-1] - 1, 0)
    tile_expert = jnp.minimum(
        jnp.sum(jnp.minimum(tile_ids, last_valid)[:, None] >= tile_end[None, :], axis=1),
        N_EXPERTS - 1).astype(jnp.int32)
    has_pad = (counts % MOE_TILE) != 0
    pad_tile = jnp.where(has_pad, tile_end - 1, n_tiles)
    zmask = jnp.zeros((n_tiles + 1,), jnp.int32).at[pad_tile].set(1)[:n_tiles]
    zmask = jnp.maximum(zmask, 1 - tile_valid)
    nz = jnp.sum(zmask).astype(jnp.int32).reshape(1)
    zorder = jnp.argsort(1 - zmask, stable=True).astype(jnp.int32)
    zrows = (zorder[:2 * N_EXPERTS] * MOE_TILE).astype(jnp.int32)
    xs, dest = _dispatch(hn2, experts, rank, start, zrows, nz, m_pad)
    yb = _experts(xs, tile_expert, tile_valid, wg, wu, wd)
    return _combine(h3, route, dest, yb)


def _stages(x, hgrn_w_in, hgrn_lb, hgrn_gnorm, hgrn_w_out, swa_w_q, swa_q_gain, swa_sink, swa_w_o, kv_norm, kv_w, k_gain, rel_bias, attn_norm, ffn_norm, ffn_w_gate, ffn_w_up, ffn_w_down, moe_router, moe_w_gate, moe_w_up, moe_w_down):
    bsz, seq, d = x.shape
    t = bsz * seq
    bf = lambda w: w.astype(BF16)
    h1 = _hgrn_layer(x, attn_norm[0], bf(hgrn_w_in[0]), hgrn_lb, hgrn_gnorm[0], bf(hgrn_w_out[0]))
    h2, q, k2, v2 = _ffn_kvq(h1.reshape(t, d), ffn_norm[0], bf(ffn_w_gate[0]), bf(ffn_w_up[0]),
                             bf(ffn_w_down[0]), kv_norm, kv_w, k_gain, attn_norm[1], bf(swa_w_q[0]))
    nk = SWA_KV_HEADS * SWA_KV_LANES
    h3, hn2, route = _attn_router(q.reshape(bsz, seq, d), k2.reshape(bsz, seq, nk),
                                  v2.reshape(bsz, seq, nk), h2.reshape(bsz, seq, d), rel_bias,
                                  swa_sink[0], swa_q_gain[0], bf(swa_w_o[0]), ffn_norm[1],
                                  moe_router[0])
    h4 = _moe_layer(h3.reshape(t, d), hn2, route.reshape(t, ROUTE_COLS),
                    bf(moe_w_gate[0]), bf(moe_w_up[0]), bf(moe_w_down[0]))
    return {"h1": h1, "h2": h2.reshape(bsz, seq, d), "h3": h3, "h4": h4.reshape(bsz, seq, d)}


def kernel(x, hgrn_w_in, hgrn_lb, hgrn_gnorm, hgrn_w_out, swa_w_q, swa_q_gain, swa_sink, swa_w_o, kv_norm, kv_w, k_gain, rel_bias, attn_norm, ffn_norm, ffn_w_gate, ffn_w_up, ffn_w_down, moe_router, moe_w_gate, moe_w_up, moe_w_down):
    return _stages(x, hgrn_w_in, hgrn_lb, hgrn_gnorm, hgrn_w_out, swa_w_q, swa_q_gain, swa_sink, swa_w_o, kv_norm, kv_w, k_gain, rel_bias, attn_norm, ffn_norm, ffn_w_gate, ffn_w_up, ffn_w_down, moe_router, moe_w_gate, moe_w_up, moe_w_down)["h4"]
```
